```python
import jax
import jax.numpy as jnp
from jax import lax
import numpy as np

D_MODEL = 1024
BATCH = 8
SEQ = 8192
DEPTH = 4
DEC_BATCH = 2
DEC_SEQ = 8192
PAST_LEN = 128

N_EVEN = (DEPTH + 1) // 2
N_ODD = DEPTH // 2
GRID_W = 64
NORM_EPS = 1e-6
ROPE_THETA = 10000.0
D_RNN = D_MODEL // 2
RG_BLOCKS = 8
RG_BW = D_RNN // RG_BLOCKS
CONV_W = 4
CONV_LEFT = 2
RG_C = 8.0
ATT_HEADS = 8
KV_HEADS = 2
HEAD_DIM = 64
Q_BLOCK = 128
D_ATT = ATT_HEADS * HEAD_DIM
D_IN_AB = 2 * D_RNN + D_ATT + 2 * KV_HEADS * HEAD_DIM
RET_HEADS = 4
RET_QK = D_MODEL // RET_HEADS
RET_V = 2 * RET_QK
RET_CHUNK = 128
D_RET_QK = RET_HEADS * RET_QK
D_RET_OUT = RET_HEADS * RET_V
D_IN_RET = 2 * D_RET_QK + 2 * D_RET_OUT
N_EXPERTS = 32
N_GROUPS = 4
EXPERTS_PER_GROUP = N_EXPERTS // N_GROUPS
TOP_K = 2
D_FF_EXPERT = 512
D_FF_SHARED = 512
MOE_BLOCK = 128

kernel_name = 'hybrid_rglru_gqa_retention_moe_encoder'


def rmsnorm(x, g=None):
    xf = x.astype(jnp.float32)
    y = xf * lax.rsqrt(jnp.mean(xf * xf, axis=-1, keepdims=True) + NORM_EPS)
    if g is not None:
        y = y * g.astype(jnp.float32)
    return y.astype(x.dtype)


def axial_rope(S, hd):
    rows = S // GRID_W
    row = jnp.repeat(jnp.arange(rows), GRID_W).astype(jnp.float32)
    col = jnp.tile(jnp.arange(GRID_W), rows).astype(jnp.float32)
    n = hd // 4
    inv = ROPE_THETA ** (-jnp.arange(n, dtype=jnp.float32) / n)
    ang = jnp.concatenate([row[:, None] * inv, col[:, None] * inv], axis=-1)
    return jnp.cos(ang), jnp.sin(ang)


def apply_rope(x, cos, sin):
    half = x.shape[-1] // 2
    x1, x2 = x[..., :half], x[..., half:]
    c = cos[None, :, None, :]
    s = sin[None, :, None, :]
    return jnp.concatenate([x1 * c - x2 * s, x2 * c + x1 * s], axis=-1).astype(x.dtype)


def _lin_combine(left, right):
    a1, b1 = left
    a2, b2 = right
    return a1 * a2, a2 * b1 + b2


def rglru_direction(xc, gw, gb, lam):
    B, S, _ = xc.shape
    xb = xc.reshape(B, S, RG_BLOCKS, RG_BW)
    gates = jnp.einsum('bsni,gnij->gbsnj', xb, gw.astype(jnp.float32)).reshape(2, B, S, D_RNN)
    gates = gates + gb.astype(jnp.float32)[:, None, None, :]
    r = jax.nn.sigmoid(gates[0])
    i = jax.nn.sigmoid(gates[1])
    log_a = -RG_C * r * jax.nn.softplus(-lam.astype(jnp.float32))
    a = jnp.exp(log_a)
    b = jnp.sqrt(-jnp.expm1(2.0 * log_a)) * (i * xc)
    _, h = lax.associative_scan(_lin_combine, (a, b), axis=1)
    return h


def rglru_bidir(x, conv_w, conv_b, gate_w, gate_b, lam):
    S = x.shape[1]
    xp = jnp.pad(x.astype(jnp.float32), ((0, 0), (CONV_LEFT, CONV_W - 1 - CONV_LEFT), (0, 0)))
    xc = conv_b.astype(jnp.float32)
    for tap in range(CONV_W):
        xc = xc + xp[:, tap:tap + S, :] * conv_w[tap].astype(jnp.float32)
    h_f = rglru_direction(xc, gate_w[0], gate_b[0], lam[0])
    h_b = jnp.flip(rglru_direction(jnp.flip(xc, 1), gate_w[1], gate_b[1], lam[1]), 1)
    return h_f + h_b


def block_attention(q, k, v):
    B, S, _, _ = q.shape
    G = ATT_HEADS // KV_HEADS
    nb = S // Q_BLOCK
    qb = q.reshape(B, nb, Q_BLOCK, KV_HEADS, G, HEAD_DIM).transpose(1, 0, 2, 3, 4, 5)
    scale = HEAD_DIM ** -0.5

    def one_block(q_blk):
        s = jnp.einsum('bqkgd,bskd->bkgqs', q_blk, k).astype(jnp.float32) * scale
        p = jax.nn.softmax(s, axis=-1)
        return jnp.einsum('bkgqs,bskd->bqkgd', p.astype(v.dtype), v)

    out = lax.map(one_block, qb)
    return out.transpose(1, 0, 2, 3, 4, 5).reshape(B, S, D_ATT)


def mixer_ab(h, w_in, conv_w, conv_b, gate_w, gate_b, lam, q_norm, k_norm, w_out):
    B, S, _ = h.shape
    proj = h @ w_in
    cuts = [D_RNN, 2 * D_RNN, 2 * D_RNN + D_ATT, 2 * D_RNN + D_ATT + KV_HEADS * HEAD_DIM]
    rg_gate, rg_x, q, k, v = jnp.split(proj, cuts, axis=-1)
    y_rg = jax.nn.gelu(rg_gate.astype(jnp.float32)) * rglru_bidir(rg_x, conv_w, conv_b, gate_w, gate_b, lam)
    cos, sin = axial_rope(S, HEAD_DIM)
    q = apply_rope(rmsnorm(q.reshape(B, S, ATT_HEADS, HEAD_DIM), q_norm), cos, sin)
    k = apply_rope(rmsnorm(k.reshape(B, S, KV_HEADS, HEAD_DIM), k_norm), cos, sin)
    v = v.reshape(B, S, KV_HEADS, HEAD_DIM)
    y_att = block_attention(q, k, v)
    y = jnp.concatenate([y_rg.astype(h.dtype), y_att.astype(h.dtype)], axis=-1)
    return (y @ w_out).astype(h.dtype)


def retention_log_decay(offset):
    return jnp.log1p(-jnp.exp2(-5.0 - offset - jnp.arange(RET_HEADS, dtype=jnp.float32)))


def retention_direction(q, k, v, log_gamma, strict):
    B, S, H, DK = q.shape
    DV = v.shape[-1]
    C = RET_CHUNK
    N = S // C
    idx = jnp.arange(C, dtype=jnp.float32)
    diff = idx[:, None] - idx[None, :]
    mask = (diff > 0) if strict else (diff >= 0)
    d_intra = jnp.where(mask, jnp.exp(jnp.where(mask, diff, 0.0)[None] * log_gamma[:, None, None]), 0.0)
    q_dec = jnp.exp((idx + 1.0)[None, :] * log_gamma[:, None]).T[None, :, :, None]
    k_dec = jnp.exp((C - 1.0 - idx)[None, :] * log_gamma[:, None]).T[None, :, :, None]
    chunk_dec = jnp.exp(C * log_gamma)[None, :, None, None]

    def to_chunks(t):
        return jnp.moveaxis(t.reshape(B, N, C, H, t.shape[-1]), 1, 0)

    def step(state, qkv):
        qc, kc, vc = qkv
        scores = jnp.einsum('bihd,bjhd->bhij', qc, kc) * d_intra[None]
        intra = jnp.einsum('bhij,bjhe->bihe', scores, vc)
        cross = jnp.einsum('bihd,bhde->bihe', qc, state) * q_dec
        state = state * chunk_dec + jnp.einsum('bjhd,bjhe->bhde', kc * k_dec, vc)
        return state, intra + cross

    state0 = jnp.zeros((B, H, DK, DV), jnp.float32)
    _, ys = lax.scan(step, state0, (to_chunks(q), to_chunks(k), to_chunks(v)))
    return jnp.moveaxis(ys, 0, 1).reshape(B, S, H, DV)


def mixer_c(h, w_in, w_out):
    B, S, _ = h.shape
    proj = h @ w_in
    q, k, v, g = jnp.split(proj, [D_RET_QK, 2 * D_RET_QK, 2 * D_RET_QK + D_RET_OUT], axis=-1)
    cos, sin = axial_rope(S, RET_QK)
    q = apply_rope(q.reshape(B, S, RET_HEADS, RET_QK).astype(jnp.float32), cos, sin)
    k = apply_rope(k.reshape(B, S, RET_HEADS, RET_QK).astype(jnp.float32), cos, sin) * (RET_QK ** -0.5)
    v = v.reshape(B, S, RET_HEADS, RET_V).astype(jnp.float32)
    y_f = retention_direction(q, k, v, retention_log_decay(0.0), False)
    y_b = jnp.flip(retention_direction(jnp.flip(q, 1), jnp.flip(k, 1), jnp.flip(v, 1),
                                       retention_log_decay(0.5), True), 1)
    y = rmsnorm(y_f + y_b).reshape(B, S, D_RET_OUT)
    y = y * jax.nn.silu(g.astype(jnp.float32))
    return (y.astype(h.dtype) @ w_out).astype(h.dtype)


def swiglu(x, wg, wu, wd):
    return (jax.nn.silu(x @ wg) * (x @ wu)) @ wd


def moe(h, router_w, router_bias, wg, wu, wd, sg, su, sd):
    B, S, D = h.shape
    T = B * S
    xt = h.reshape(T, D)
    scores = jax.nn.sigmoid((xt @ router_w).astype(jnp.float32))
    biased = scores + router_bias.astype(jnp.float32)
    grouped = biased.reshape(T, N_GROUPS, EXPERTS_PER_GROUP)
    group_score = lax.top_k(grouped, 2)[0].sum(-1)
    g_sel = jnp.argmax(group_score, axis=-1)
    in_group = grouped[jnp.arange(T), g_sel]
    _, local_idx = lax.top_k(in_group, TOP_K)
    expert_idx = g_sel[:, None] * EXPERTS_PER_GROUP + local_idx
    w = jnp.take_along_axis(scores, expert_idx, axis=1)
    w = w / jnp.sum(w, axis=-1, keepdims=True)
    A = T * TOP_K
    e_flat = expert_idx.reshape(A)
    tok_flat = jnp.repeat(jnp.arange(T), TOP_K)
    e_s, tok_s, g_s = lax.sort((e_flat, tok_flat, w.reshape(A)), num_keys=1)
    counts = jax.ops.segment_sum(jnp.ones_like(e_flat), e_flat, num_segments=N_EXPERTS)
    starts = jnp.cumsum(counts) - counts
    padded = ((counts + MOE_BLOCK - 1) // MOE_BLOCK) * MOE_BLOCK
    pad_end = jnp.cumsum(padded)
    pad_start = pad_end - padded
    dest = pad_start[e_s] + (jnp.arange(A) - starts[e_s])
    P = ((A + MOE_BLOCK - 1) // MOE_BLOCK) * MOE_BLOCK + N_EXPERTS * MOE_BLOCK
    n_blocks = P // MOE_BLOCK
    x_buf = jnp.zeros((P, D), h.dtype).at[dest].set(xt[tok_s])
    block_expert = jnp.minimum(
        jnp.searchsorted(pad_end, jnp.arange(n_blocks) * MOE_BLOCK, side='right'), N_EXPERTS - 1)

    def run_block(args):
        xb, e = args
        return swiglu(xb, wg[e], wu[e], wd[e])

    y_buf = lax.map(run_block, (x_buf.reshape(n_blocks, MOE_BLOCK, D), block_expert)).reshape(P, D)
    y = jnp.zeros((T, D), jnp.float32).at[tok_s].add(y_buf[dest].astype(jnp.float32) * g_s[:, None])
    y = y + swiglu(xt, sg, su, sd).astype(jnp.float32)
    return y.reshape(B, S, D).astype(h.dtype)


def setup_inputs(seed: int = 0) -> dict:
    key = jax.random.key(seed)
    ks = iter(jax.random.split(key, 40))

    def nrm(shape, scale):
        return jax.random.normal(next(ks), shape, jnp.float32) * scale

    D = D_MODEL
    x_prompt = nrm((BATCH, SEQ, D), 1.0)
    x_sample = nrm((DEC_BATCH, DEC_SEQ, D), 1.0)
    c_prompt = nrm((BATCH, D), 1.0)
    c_sample = nrm((DEC_BATCH, D), 1.0)
    norm_mix = 1.0 + nrm((DEPTH, D), 0.1)
    norm_ffn = 1.0 + nrm((DEPTH, D), 0.1)
    w_ada = nrm((DEPTH, D, 6 * D), 0.5 * D ** -0.5)
    b_ada = nrm((DEPTH, 6 * D), 0.02)
    w_in_ab = nrm((N_EVEN, D, D_IN_AB), D ** -0.5)
    conv_w = nrm((N_EVEN, CONV_W, D_RNN), 0.5)
    conv_b = nrm((N_EVEN, D_RNN), 0.02)
    rg_gate_w = nrm((N_EVEN, 2, 2, RG_BLOCKS, RG_BW, RG_BW), RG_BW ** -0.5)
    rg_gate_b = nrm((N_EVEN, 2, 2, D_RNN), 0.02)
    u = jax.random.uniform(next(ks), (N_EVEN, 2, D_RNN), jnp.float32, minval=0.9, maxval=0.999)
    a0 = u ** (1.0 / RG_C)
    rg_lambda = jnp.log(a0) - jnp.log1p(-a0)
    q_norm = 1.0 + nrm((N_EVEN, HEAD_DIM), 0.1)
    k_norm = 1.0 + nrm((N_EVEN, HEAD_DIM), 0.1)
    w_out_ab = nrm((N_EVEN, D_RNN + D_ATT, D), (D_RNN + D_ATT) ** -0.5)
    w_in_ret = nrm((N_ODD, D, D_IN_RET), D ** -0.5)
    w_out_ret = nrm((N_ODD, D_RET_OUT, D), D_RET_OUT ** -0.5)
    router_w = nrm((D, N_EXPERTS), D ** -0.5)
    router_bias = nrm((N_EXPERTS,), 0.01)
    exp_w_gate = nrm((DEPTH, N_EXPERTS, D, D_FF_EXPERT), D ** -0.5)
    exp_w_up = nrm((DEPTH, N_EXPERTS, D, D_FF_EXPERT), D ** -0.5)
    exp_w_down = nrm((DEPTH, N_EXPERTS, D_FF_EXPERT, D), D_FF_EXPERT ** -0.5)
    sh_w_gate = nrm((DEPTH, D, D_FF_SHARED), D ** -0.5)
    sh_w_up = nrm((DEPTH, D, D_FF_SHARED), D ** -0.5)
    sh_w_down = nrm((DEPTH, D_FF_SHARED, D), D_FF_SHARED ** -0.5)
    return {'x_prompt': x_prompt, 'x_sample': x_sample, 'c_prompt': c_prompt, 'c_sample': c_sample,
            'norm_mix': norm_mix, 'norm_ffn': norm_ffn, 'w_ada': w_ada, 'b_ada': b_ada,
            'w_in_ab': w_in_ab, 'conv_w': conv_w, 'conv_b': conv_b, 'rg_gate_w': rg_gate_w,
            'rg_gate_b': rg_gate_b, 'rg_lambda': rg_lambda, 'q_norm': q_norm, 'k_norm': k_norm,
            'w_out_ab': w_out_ab, 'w_in_ret': w_in_ret, 'w_out_ret': w_out_ret,
            'router_w': router_w, 'router_bias': router_bias, 'exp_w_gate': exp_w_gate,
            'exp_w_up': exp_w_up, 'exp_w_down': exp_w_down, 'sh_w_gate': sh_w_gate,
            'sh_w_up': sh_w_up, 'sh_w_down': sh_w_down}


def reference(x_prompt, x_sample, c_prompt, c_sample, norm_mix, norm_ffn, w_ada, b_ada,
              w_in_ab, conv_w, conv_b, rg_gate_w, rg_gate_b, rg_lambda, q_norm, k_norm,
              w_out_ab, w_in_ret, w_out_ret, router_w, router_bias, exp_w_gate, exp_w_up,
              exp_w_down, sh_w_gate, sh_w_up, sh_w_down):
    def trunk(x, c):
        c_act = jax.nn.silu(c)
        for l in range(DEPTH):
            mod = c_act @ w_ada[l] + b_ada[l]
            shift1, scale1, gate1, shift2, scale2, gate2 = jnp.split(mod[:, None, :], 6, axis=-1)
            h = (rmsnorm(x, norm_mix[l]) * (1.0 + scale1) + shift1).astype(x.dtype)
            if l % 2 == 0:
                e = l // 2
                y = mixer_ab(h, w_in_ab[e], conv_w[e], conv_b[e], rg_gate_w[e], rg_gate_b[e],
                             rg_lambda[e], q_norm[e], k_norm[e], w_out_ab[e])
            else:
                o = l // 2
                y = mixer_c(h, w_in_ret[o], w_out_ret[o])
            x = (x + gate1 * y).astype(x.dtype)
            h = (rmsnorm(x, norm_ffn[l]) * (1.0 + scale2) + shift2).astype(x.dtype)
            y = moe(h, router_w, router_bias, exp_w_gate[l], exp_w_up[l], exp_w_down[l],
                    sh_w_gate[l], sh_w_up[l], sh_w_down[l])
            x = (x + gate2 * y).astype(x.dtype)
        return x

    y_prompt = trunk(x_prompt, c_prompt)
    y_sample = trunk(x_sample, c_sample)
    return (y_prompt, y_sample)
```

```python
import functools

import jax
import jax.numpy as jnp
from jax import lax
from jax.experimental import pallas as pl
from jax.experimental.pallas import tpu as pltpu

F32 = jnp.float32
BF16 = jnp.bfloat16

D_MODEL = 1024
DEPTH = 4
GRID_W = 64
NORM_EPS = 1e-6
ROPE_THETA = 10000.0
D_RNN = 512
RG_BW = 64
CONV_W = 4
RG_C = 8.0
ATT_HEADS = 8
KV_HEADS = 2
HEAD_DIM = 64
D_ATT = ATT_HEADS * HEAD_DIM
RET_HEADS = 4
RET_QK = 256
RET_V = 512
RET_CHUNK = 128
N_EXPERTS = 32
N_GROUPS = 4
EPG = N_EXPERTS // N_GROUPS
TOP_K = 2
D_FF = 512

LANES = 128
SUBLANES = 8
VMEM_LIMIT = 52 * 1024 * 1024

ROW_TILE = 256
ATT_TQ = 128
ATT_TK = 1024
SCAN_ROWS = 128
RET_STEP_ROWS = 1024
MOE_BM = 512
COMB_TM = 256
NEG_BIG = -1e30


def _cparams(sem):
    return pltpu.CompilerParams(dimension_semantics=sem, vmem_limit_bytes=VMEM_LIMIT)


def _dot(a, b):
    return jnp.dot(a, b, preferred_element_type=F32)


def _silu(x):
    return x * jax.nn.sigmoid(x)


def _norm_mod(x, g, scale, shift):
    ms = jnp.mean(x * x, axis=-1, keepdims=True)
    h = x * lax.rsqrt(ms + NORM_EPS) * g
    return h * (1.0 + scale) + shift


def _ada_kernel(c_ref, w_ref, b_ref, o_ref):
    c = c_ref[...]
    o_ref[0] = _dot(_silu(c).astype(BF16), w_ref[0].astype(BF16)) + b_ref[0]


def _ada_mod(c_pad, w_ada, b_ada):
    L, D, N = w_ada.shape
    Bp = c_pad.shape[0]
    tn = D_MODEL
    return pl.pallas_call(
        _ada_kernel,
        grid=(L, N // tn),
        in_specs=[
            pl.BlockSpec((Bp, D), lambda l, j: (0, 0)),
            pl.BlockSpec((1, D, tn), lambda l, j: (l, 0, j)),
            pl.BlockSpec((1, 1, tn), lambda l, j: (l, 0, j)),
        ],
        out_specs=pl.BlockSpec((1, Bp, tn), lambda l, j: (l, 0, j)),
        out_shape=jax.ShapeDtypeStruct((L, Bp, N), F32),
        compiler_params=_cparams(("arbitrary", "arbitrary")),
        name="ada",
    )(c_pad, w_ada, b_ada.reshape(L, 1, N))


def _segment_mean_sq(x, ones_bd, width):
    sq = x * x
    hi = sq.astype(BF16)
    lo = (sq - hi.astype(F32)).astype(BF16)
    return (_dot(hi, ones_bd) + _dot(lo, ones_bd)) * (1.0 / width)


def _rope_halves(x, cos2, sin_signed, half):
    w = x.shape[-1]
    lane = lax.broadcasted_iota(jnp.int32, x.shape, 1)
    first = (lane % (2 * half)) < half
    rot = jnp.where(first, pltpu.roll(x, w - half, 1), pltpu.roll(x, half, 1))
    return x * cos2 + rot * sin_signed


def _inproj_even_kernel(x_ref, g_ref, sc_ref, sh_ref, w_ref, ones_ref, qn_ref, kn_ref,
                        cos_ref, sin_ref, rg_ref, q_ref, k_ref, v_ref):
    h = _norm_mod(x_ref[0], g_ref[...], sc_ref[0], sh_ref[0]).astype(BF16)
    rg_ref[0] = _dot(h, w_ref[:, 0:2 * D_RNN])
    c0 = 2 * D_RNN
    q = _dot(h, w_ref[:, c0:c0 + D_ATT])
    c1 = c0 + D_ATT
    kw = KV_HEADS * HEAD_DIM
    k = _dot(h, w_ref[:, c1:c1 + kw])
    v_ref[0] = _dot(h, w_ref[:, c1 + kw:c1 + 2 * kw]).astype(BF16)

    half = HEAD_DIM // 2
    ones = ones_ref[...]
    qn = q * lax.rsqrt(_segment_mean_sq(q, ones, HEAD_DIM) + NORM_EPS) * qn_ref[...]
    qr = _rope_halves(qn, cos_ref[...], sin_ref[...], half)
    q_ref[0] = (qr * (HEAD_DIM ** -0.5)).astype(BF16)
    kn = k * lax.rsqrt(_segment_mean_sq(k, ones[:kw, :kw], HEAD_DIM) + NORM_EPS) * kn_ref[...]
    kr = _rope_halves(kn, cos_ref[:, :kw], sin_ref[:, :kw], half)
    k_ref[0] = kr.astype(BF16)


def _inproj_even(x, g, scale, shift, w, ones_bd, qn, kn, cos_t, sin_t):
    B, S, D = x.shape
    tm = ROW_TILE
    N = w.shape[1]
    kw = KV_HEADS * HEAD_DIM
    row = lambda b, i: (b, i, 0)
    per_b = lambda b, i: (b, 0, 0)
    const2 = lambda b, i: (0, 0)
    return pl.pallas_call(
        _inproj_even_kernel,
        grid=(B, S // tm),
        in_specs=[
            pl.BlockSpec((1, tm, D), row),
            pl.BlockSpec((1, D), const2),
            pl.BlockSpec((1, 1, D), per_b),
            pl.BlockSpec((1, 1, D), per_b),
            pl.BlockSpec((D, N), const2),
            pl.BlockSpec((D_ATT, D_ATT), const2),
            pl.BlockSpec((1, D_ATT), const2),
            pl.BlockSpec((1, kw), const2),
            pl.BlockSpec((tm, D_ATT), lambda b, i: (i, 0)),
            pl.BlockSpec((tm, D_ATT), lambda b, i: (i, 0)),
        ],
        out_specs=[
            pl.BlockSpec((1, tm, 2 * D_RNN), row),
            pl.BlockSpec((1, tm, D_ATT), row),
            pl.BlockSpec((1, tm, kw), row),
            pl.BlockSpec((1, tm, kw), row),
        ],
        out_shape=[
            jax.ShapeDtypeStruct((B, S, 2 * D_RNN), F32),
            jax.ShapeDtypeStruct((B, S, D_ATT), BF16),
            jax.ShapeDtypeStruct((B, S, kw), BF16),
            jax.ShapeDtypeStruct((B, S, kw), BF16),
        ],
        compiler_params=_cparams(("arbitrary", "arbitrary")),
        name="inproj_even",
    )(x, g, scale, shift, w, ones_bd, qn, kn, cos_t, sin_t)


def _scan_chunk(a, b, reverse):
    R = a.shape[0]
    row = lax.broadcasted_iota(jnp.int32, a.shape, 0)
    s = 1
    while s < R:
        if reverse:
            keep = row < R - s
            shift = R - s
        else:
            keep = row >= s
            shift = s
        a_sh = jnp.where(keep, pltpu.roll(a, shift, 0), 1.0)
        b_sh = jnp.where(keep, pltpu.roll(b, shift, 0), 0.0)
        b = a * b_sh + b
        a = a * a_sh
        s *= 2
    return a, b


def _rglru_kernel(x_ref, gate_ref, cw_ref, cb_ref, gw_ref, gb_ref, lam_ref, y_ref, hf_ref, carry_ref):
    S = x_ref.shape[1]
    R = SCAN_ROWS
    n_chunks = S // R
    cw = cw_ref[...]
    cb = cb_ref[...]

    def conv_chunk(c):
        r0 = pl.multiple_of(c * R, R)
        main = x_ref[0, pl.ds(r0, R), :]
        lo = pl.multiple_of(jnp.maximum(r0 - SUBLANES, 0), SUBLANES)
        hi = pl.multiple_of(jnp.minimum(r0 + R, S - SUBLANES), SUBLANES)
        prev = jnp.where(c > 0, x_ref[0, pl.ds(lo, SUBLANES), :], 0.0)
        nxt = jnp.where(c < n_chunks - 1, x_ref[0, pl.ds(hi, SUBLANES), :], 0.0)
        ext = jnp.concatenate([prev, main, nxt], axis=0)
        xc = cb
        for tap in range(CONV_W):
            o = SUBLANES - 2 + tap
            xc = xc + ext[o:o + R, :] * cw[tap:tap + 1, :]
        return r0, xc

    def direction(c, d, reverse):
        r0, xc = conv_chunk(c)
        xb = xc.astype(BF16)
        r = jax.nn.sigmoid(_dot(xb, gw_ref[d, 0, 0]) + gb_ref[d, 0])
        i = jax.nn.sigmoid(_dot(xb, gw_ref[d, 1, 0]) + gb_ref[d, 1])
        log_a = -RG_C * r * jax.nn.softplus(-lam_ref[d])
        a = jnp.exp(log_a)
        b = jnp.sqrt(jnp.tanh(-log_a) * (a * a + 1.0)) * (i * xc)
        a_cum, h0 = _scan_chunk(a, b, reverse)
        h = a_cum * carry_ref[...] + h0
        carry_ref[...] = h[0:1, :] if reverse else h[R - 1:R, :]
        return r0, h

    carry_ref[...] = jnp.zeros_like(carry_ref)

    def fwd_body(c, _):
        r0, h = direction(c, 0, False)
        hf_ref[pl.ds(r0, R), :] = h
        return 0

    lax.fori_loop(0, n_chunks, fwd_body, 0)
    carry_ref[...] = jnp.zeros_like(carry_ref)

    def bwd_body(j, _):
        c = n_chunks - 1 - j
        r0, h = direction(c, 1, True)
        gate = gate_ref[0, pl.ds(r0, R), :]
        y = jax.nn.gelu(gate) * (hf_ref[pl.ds(r0, R), :] + h)
        y_ref[0, pl.ds(r0, R), :] = y.astype(y_ref.dtype)
        return 0

    lax.fori_loop(0, n_chunks, bwd_body, 0)


def _rglru(rg, conv_w, conv_b, gw_bd, gate_b, lam):
    B, S, _ = rg.shape
    n_lg = D_RNN // LANES
    return pl.pallas_call(
        _rglru_kernel,
        grid=(B, n_lg),
        in_specs=[
            pl.BlockSpec((1, S, LANES), lambda b, j: (b, 0, n_lg + j)),
            pl.BlockSpec((1, S, LANES), lambda b, j: (b, 0, j)),
            pl.BlockSpec((CONV_W, LANES), lambda b, j: (0, j)),
            pl.BlockSpec((1, LANES), lambda b, j: (0, j)),
            pl.BlockSpec((2, 2, 1, LANES, LANES), lambda b, j: (0, 0, j, 0, 0)),
            pl.BlockSpec((2, 2, 1, LANES), lambda b, j: (0, 0, 0, j)),
            pl.BlockSpec((2, 1, LANES), lambda b, j: (0, 0, j)),
        ],
        out_specs=pl.BlockSpec((1, S, LANES), lambda b, j: (b, 0, j)),
        out_shape=jax.ShapeDtypeStruct((B, S, D_RNN), BF16),
        scratch_shapes=[pltpu.VMEM((S, LANES), F32), pltpu.VMEM((1, LANES), F32)],
        compiler_params=_cparams(("arbitrary", "arbitrary")),
        name="rglru",
    )(rg, rg, conv_w, conv_b, gw_bd, gate_b, lam)


def _attn_kernel(q_ref, kt_ref, v_ref, o_ref, m_ref, l_ref, acc_ref):
    S = kt_ref.shape[2]
    tq = q_ref.shape[1]
    G = ATT_HEADS // KV_HEADS
    kvh = pl.program_id(1)
    q = q_ref[0]
    q4 = jnp.concatenate([q[:, g * HEAD_DIM:(g + 1) * HEAD_DIM] for g in range(G)], axis=0)
    m_ref[...] = jnp.full_like(m_ref, NEG_BIG)
    l_ref[...] = jnp.zeros_like(l_ref)
    acc_ref[...] = jnp.zeros_like(acc_ref)

    def body(j, _):
        k0 = pl.multiple_of(j * ATT_TK, ATT_TK)
        s = _dot(q4, kt_ref[0, :, pl.ds(k0, ATT_TK)])
        m_old = m_ref[...]
        m_new = jnp.maximum(m_old, jnp.max(s, axis=1, keepdims=True))
        alpha = jnp.exp(m_old - m_new)
        p = jnp.exp(s - m_new)
        l_ref[...] = alpha * l_ref[...] + jnp.sum(p, axis=1, keepdims=True)
        acc_ref[...] = alpha * acc_ref[...] + _dot(p.astype(BF16), v_ref[0, pl.ds(k0, ATT_TK), :])
        m_ref[...] = m_new
        return 0

    lax.fori_loop(0, S // ATT_TK, body, 0)
    o = acc_ref[...] / l_ref[...]
    o = jnp.where(kvh == 0, o[:, :HEAD_DIM], o[:, HEAD_DIM:])
    out = jnp.concatenate([o[g * tq:(g + 1) * tq, :] for g in range(G)], axis=1)
    o_ref[0] = out.astype(o_ref.dtype)


def _attention(q, kt, v):
    B, S, _ = q.shape
    G = ATT_HEADS // KV_HEADS
    tq = ATT_TQ
    qw = G * HEAD_DIM
    kw = KV_HEADS * HEAD_DIM
    return pl.pallas_call(
        _attn_kernel,
        grid=(B, KV_HEADS, S // tq),
        in_specs=[
            pl.BlockSpec((1, tq, qw), lambda b, h, i: (b, i, h)),
            pl.BlockSpec((1, HEAD_DIM, S), lambda b, h, i: (b, h, 0)),
            pl.BlockSpec((1, S, kw), lambda b, h, i: (b, 0, 0)),
        ],
        out_specs=pl.BlockSpec((1, tq, qw), lambda b, h, i: (b, i, h)),
        out_shape=jax.ShapeDtypeStruct((B, S, D_ATT), BF16),
        scratch_shapes=[
            pltpu.VMEM((G * tq, 1), F32),
            pltpu.VMEM((G * tq, 1), F32),
            pltpu.VMEM((G * tq, kw), F32),
        ],
        compiler_params=_cparams(("arbitrary", "arbitrary", "arbitrary")),
        name="attn",
    )(q, kt, v)


def _outproj_kernel(*refs, n_in):
    y_refs = refs[:n_in]
    w_ref, x_ref, gate_ref, o_ref = refs[n_in:]
    acc = None
    off = 0
    for yr in y_refs:
        kd = yr.shape[-1]
        part = _dot(yr[0], w_ref[off:off + kd, :])
        acc = part if acc is None else acc + part
        off += kd
    o_ref[0] = x_ref[0] + gate_ref[0] * acc


def _outproj(ys, w, x, gate):
    B, S, D = x.shape
    tm = ROW_TILE
    row = lambda b, i: (b, i, 0)
    in_specs = [pl.BlockSpec((1, tm, y.shape[-1]), row) for y in ys]
    in_specs += [
        pl.BlockSpec(w.shape, lambda b, i: (0, 0)),
        pl.BlockSpec((1, tm, D), row),
        pl.BlockSpec((1, 1, D), lambda b, i: (b, 0, 0)),
    ]
    return pl.pallas_call(
        functools.partial(_outproj_kernel, n_in=len(ys)),
        grid=(B, S // tm),
        in_specs=in_specs,
        out_specs=pl.BlockSpec((1, tm, D), row),
        out_shape=jax.ShapeDtypeStruct((B, S, D), F32),
        compiler_params=_cparams(("arbitrary", "arbitrary")),
        name="outproj",
    )(*ys, w, x, gate)


def _inproj_ret_kernel(x_ref, g_ref, sc_ref, sh_ref, w_ref, cos_ref, sin_ref,
                       q_ref, k_ref, v_ref, gate_ref):
    h = _norm_mod(x_ref[0], g_ref[...], sc_ref[0], sh_ref[0]).astype(BF16)
    cos = cos_ref[...]
    sin = sin_ref[...]
    half = RET_QK // 2
    dqk = RET_HEADS * RET_QK
    dv = RET_HEADS * RET_V

    def rope(t):
        t1 = t[:, :half]
        t2 = t[:, half:]
        return jnp.concatenate([t1 * cos - t2 * sin, t2 * cos + t1 * sin], axis=-1)

    for hd in range(RET_HEADS):
        c = hd * RET_QK
        q_ref[0, :, c:c + RET_QK] = rope(_dot(h, w_ref[:, c:c + RET_QK])).astype(BF16)
        kh = rope(_dot(h, w_ref[:, dqk + c:dqk + c + RET_QK])) * (RET_QK ** -0.5)
        k_ref[0, :, c:c + RET_QK] = kh.astype(BF16)
    for hd in range(RET_HEADS):
        c = hd * RET_V
        v_ref[0, :, c:c + RET_V] = _dot(h, w_ref[:, 2 * dqk + c:2 * dqk + c + RET_V]).astype(BF16)
        gate_ref[0, :, c:c + RET_V] = _dot(h, w_ref[:, 2 * dqk + dv + c:2 * dqk + dv + c + RET_V])


def _inproj_ret(x, g, scale, shift, w, cos_t, sin_t):
    B, S, D = x.shape
    tm = ROW_TILE
    N = w.shape[1]
    dqk = RET_HEADS * RET_QK
    dv = RET_HEADS * RET_V
    half = RET_QK // 2
    row = lambda b, i: (b, i, 0)
    per_b = lambda b, i: (b, 0, 0)
    const2 = lambda b, i: (0, 0)
    return pl.pallas_call(
        _inproj_ret_kernel,
        grid=(B, S // tm),
        in_specs=[
            pl.BlockSpec((1, tm, D), row),
            pl.BlockSpec((1, D), const2),
            pl.BlockSpec((1, 1, D), per_b),
            pl.BlockSpec((1, 1, D), per_b),
            pl.BlockSpec((D, N), const2),
            pl.BlockSpec((tm, half), lambda b, i: (i, 0)),
            pl.BlockSpec((tm, half), lambda b, i: (i, 0)),
        ],
        out_specs=[
            pl.BlockSpec((1, tm, dqk), row),
            pl.BlockSpec((1, tm, dqk), row),
            pl.BlockSpec((1, tm, dv), row),
            pl.BlockSpec((1, tm, dv), row),
        ],
        out_shape=[
            jax.ShapeDtypeStruct((B, S, dqk), BF16),
            jax.ShapeDtypeStruct((B, S, dqk), BF16),
            jax.ShapeDtypeStruct((B, S, dv), BF16),
            jax.ShapeDtypeStruct((B, S, dv), F32),
        ],
        compiler_params=_cparams(("arbitrary", "arbitrary")),
        name="inproj_ret",
    )(x, g, scale, shift, w, cos_t, sin_t)


def _ret_kernel(*refs, reverse):
    if reverse:
        q_ref, kt_ref, v_ref, dm_ref, qd_ref, kd_ref, cd_ref, yf_ref, g_ref, o_ref, st_ref = refs
    else:
        q_ref, kt_ref, v_ref, dm_ref, qd_ref, kd_ref, cd_ref, o_ref, st_ref = refs
    C = RET_CHUNK
    n_sub = q_ref.shape[1] // C

    @pl.when(pl.program_id(2) == 0)
    def _():
        st_ref[...] = jnp.zeros_like(st_ref)

    dm = dm_ref[0]
    qd = qd_ref[0]
    kd = kd_ref[0]
    cd = cd_ref[0]
    order = range(n_sub - 1, -1, -1) if reverse else range(n_sub)
    for ci in order:
        r = slice(ci * C, (ci + 1) * C)
        qc = q_ref[0, r, :]
        ktc = kt_ref[0, 0, :, r]
        vc = v_ref[0, r, :]
        st = st_ref[...]
        scores = _dot(qc, ktc) * dm
        out = _dot(scores.astype(BF16), vc) + _dot(qc, st.astype(BF16)) * qd
        st_ref[...] = st * cd + _dot((ktc.astype(F32) * kd).astype(BF16), vc)
        if reverse:
            y = yf_ref[0, r, :] + out
            y = y * lax.rsqrt(jnp.mean(y * y, axis=-1, keepdims=True) + NORM_EPS)
            o_ref[0, r, :] = (y * _silu(g_ref[0, r, :])).astype(o_ref.dtype)
        else:
            o_ref[0, r, :] = out


def _retention_pass(q, kt, v, tabs, reverse, yf=None, gate=None):
    B, S, _ = q.shape
    ts = min(RET_STEP_ROWS, S)
    n = S // ts
    C = RET_CHUNK
    step = (lambda i: n - 1 - i) if reverse else (lambda i: i)
    rows = lambda width: pl.BlockSpec((1, ts, width), lambda b, h, i: (b, step(i), h))
    per_h = lambda shape: pl.BlockSpec((1,) + shape, lambda b, h, i: (h, 0, 0))
    in_specs = [
        rows(RET_QK),
        pl.BlockSpec((1, 1, RET_QK, ts), lambda b, h, i: (b, h, 0, step(i))),
        rows(RET_V),
        per_h((C, C)), per_h((C, 1)), per_h((1, C)), per_h((1, 1)),
    ]
    args = [q, kt, v, *tabs]
    if reverse:
        in_specs += [rows(RET_V), rows(RET_V)]
        args += [yf, gate]
    return pl.pallas_call(
        functools.partial(_ret_kernel, reverse=reverse),
        grid=(B, RET_HEADS, n),
        in_specs=in_specs,
        out_specs=rows(RET_V),
        out_shape=jax.ShapeDtypeStruct((B, S, RET_HEADS * RET_V), BF16 if reverse else F32),
        scratch_shapes=[pltpu.VMEM((RET_QK, RET_V), F32)],
        compiler_params=_cparams(("arbitrary", "arbitrary", "arbitrary")),
        name="retention_bwd" if reverse else "retention_fwd",
    )(*args)


def _retention_tables(offset, strict):
    C = RET_CHUNK
    log_gamma = jnp.log1p(-jnp.exp2(-5.0 - offset - jnp.arange(RET_HEADS, dtype=F32)))
    idx = jnp.arange(C, dtype=F32)
    diff = idx[:, None] - idx[None, :]
    mask = (diff > 0) if strict else (diff >= 0)
    d_intra = jnp.where(mask, jnp.exp(jnp.where(mask, diff, 0.0)[None] * log_gamma[:, None, None]), 0.0)
    q_dec = jnp.exp((idx + 1.0)[None, :] * log_gamma[:, None])
    k_dec = jnp.exp((C - 1.0 - idx)[None, :] * log_gamma[:, None])
    chunk_dec = jnp.exp(C * log_gamma)
    return d_intra, q_dec, k_dec, chunk_dec


def _shape_tables(d_intra, q_dec, k_dec, chunk_dec, flip):
    if flip:
        d_intra = d_intra[:, ::-1, ::-1]
        q_dec = q_dec[:, ::-1]
        k_dec = k_dec[:, ::-1]
    return (d_intra, q_dec[:, :, None], k_dec[:, None, :], chunk_dec[:, None, None])


def _first_argmax(x, iota, size):
    m = jnp.max(x, axis=0, keepdims=True)
    return m, jnp.min(jnp.where(x == m, iota, size), axis=0, keepdims=True)


def _moe_pre_kernel(x_ref, g_ref, sc_ref, sh_ref, gate_ref, rwt_ref, rb_ref, sg_ref, su_ref, sd_ref,
                    h2_ref, xb_ref, idx_ref, wt_ref):
    x = x_ref[0]
    h = _norm_mod(x, g_ref[...], sc_ref[0], sh_ref[0])
    h2_ref[0] = h
    hb = h.astype(BF16)
    tm = x.shape[0]

    logits = lax.dot_general(rwt_ref[...], hb, (((1,), (1,)), ((), ())), preferred_element_type=F32)
    scores = jax.nn.sigmoid(logits)
    biased = scores + rb_ref[...]
    iota_e = lax.broadcasted_iota(jnp.int32, (EPG, tm), 0).astype(F32)
    group_scores = []
    for gi in range(N_GROUPS):
        bg = biased[gi * EPG:(gi + 1) * EPG, :]
        m1, i1 = _first_argmax(bg, iota_e, EPG)
        m2 = jnp.max(jnp.where(iota_e == i1, -jnp.inf, bg), axis=0, keepdims=True)
        group_scores.append(m1 + m2)
    gs = jnp.concatenate(group_scores, axis=0)
    iota_g = lax.broadcasted_iota(jnp.int32, (N_GROUPS, tm), 0).astype(F32)
    _, g_sel = _first_argmax(gs, iota_g, N_GROUPS)
    in_b = jnp.zeros((EPG, tm), F32)
    in_s = jnp.zeros((EPG, tm), F32)
    for gi in range(N_GROUPS):
        sel = g_sel == gi
        in_b = jnp.where(sel, biased[gi * EPG:(gi + 1) * EPG, :], in_b)
        in_s = jnp.where(sel, scores[gi * EPG:(gi + 1) * EPG, :], in_s)
    _, l1 = _first_argmax(in_b, iota_e, EPG)
    _, l2 = _first_argmax(jnp.where(iota_e == l1, -jnp.inf, in_b), iota_e, EPG)
    w1 = jnp.sum(jnp.where(iota_e == l1, in_s, 0.0), axis=0, keepdims=True)
    w2 = jnp.sum(jnp.where(iota_e == l2, in_s, 0.0), axis=0, keepdims=True)
    wsum = w1 + w2
    idx_ref[...] = jnp.concatenate([g_sel * EPG + l1, g_sel * EPG + l2], axis=0).astype(jnp.int32)
    wt_ref[...] = jnp.concatenate([w1 / wsum, w2 / wsum], axis=0)

    act = _silu(_dot(hb, sg_ref[...])) * _dot(hb, su_ref[...])
    xb_ref[0] = x + gate_ref[0] * _dot(act.astype(BF16), sd_ref[...])


def _moe_pre(x, g, scale, shift, gate, rwt, rb, sg, su, sd):
    B, S, D = x.shape
    tm = ROW_TILE
    n = S // tm
    row = lambda b, i: (b, i, 0)
    per_b = lambda b, i: (b, 0, 0)
    const2 = lambda b, i: (0, 0)
    tok = lambda b, i: (0, b * n + i)
    return pl.pallas_call(
        _moe_pre_kernel,
        grid=(B, n),
        in_specs=[
            pl.BlockSpec((1, tm, D), row),
            pl.BlockSpec((1, D), const2),
            pl.BlockSpec((1, 1, D), per_b),
            pl.BlockSpec((1, 1, D), per_b),
            pl.BlockSpec((1, 1, D), per_b),
            pl.BlockSpec((N_EXPERTS, D), const2),
            pl.BlockSpec((N_EXPERTS, 1), const2),
            pl.BlockSpec((D, D_FF), const2),
            pl.BlockSpec((D, D_FF), const2),
            pl.BlockSpec((D_FF, D), const2),
        ],
        out_specs=[
            pl.BlockSpec((1, tm, D), row),
            pl.BlockSpec((1, tm, D), row),
            pl.BlockSpec((TOP_K, tm), tok),
            pl.BlockSpec((TOP_K, tm), tok),
        ],
        out_shape=[
            jax.ShapeDtypeStruct((B, S, D), F32),
            jax.ShapeDtypeStruct((B, S, D), F32),
            jax.ShapeDtypeStruct((TOP_K, B * S), jnp.int32),
            jax.ShapeDtypeStruct((TOP_K, B * S), F32),
        ],
        compiler_params=_cparams(("arbitrary", "arbitrary")),
        name="moe_pre",
    )(x, g, scale, shift, gate, rwt, rb, sg, su, sd)


def _row_gather_copy(src_hbm, row, dst, slot, r, sem):
    return pltpu.make_async_copy(src_hbm.at[pl.ds(row, 1), :], dst.at[slot, pl.ds(r, 1), :], sem.at[slot])


def _expert_kernel(be_ref, nb_ref, tok_cur_ref, tok_next_ref, h_hbm, w_ref, wg_ref, wu_ref, wd_ref,
                   o_ref, xbuf, sem):
    i = pl.program_id(0)
    slot = i % 2
    n_used = nb_ref[0]
    bm = xbuf.shape[1]

    def issue(tok_ref, dst_slot):
        def body(r, _):
            _row_gather_copy(h_hbm, tok_ref[0, 0, r], xbuf, dst_slot, r, sem).start()
            return 0
        lax.fori_loop(0, bm, body, 0)

    @pl.when(i == 0)
    def _():
        issue(tok_cur_ref, 0)

    @pl.when(i + 1 < n_used)
    def _():
        issue(tok_next_ref, 1 - slot)

    @pl.when(i < n_used)
    def _():
        def wait_body(r, _):
            _row_gather_copy(h_hbm, 0, xbuf, slot, r, sem).wait()
            return 0
        lax.fori_loop(0, bm, wait_body, 0)
        xb = xbuf[slot].astype(BF16)
        act = _silu(_dot(xb, wg_ref[0])) * _dot(xb, wu_ref[0])
        o_ref[...] = _dot(act.astype(BF16), wd_ref[0]) * w_ref[...]

    @pl.when(i >= n_used)
    def _():
        o_ref[...] = jnp.zeros_like(o_ref)


def _experts(block_expert, n_used, tok_disp, h2, w_disp, wg, wu, wd):
    P = w_disp.shape[0]
    bm = MOE_BM
    nb = P // bm
    D = h2.shape[-1]
    tok3 = tok_disp.reshape(nb, 1, bm)
    smem_blk = lambda f: pl.BlockSpec((1, 1, bm), f, memory_space=pltpu.SMEM)
    grid_spec = pltpu.PrefetchScalarGridSpec(
        num_scalar_prefetch=2,
        grid=(nb,),
        in_specs=[
            smem_blk(lambda i, be, nu: (i, 0, 0)),
            smem_blk(lambda i, be, nu: (jnp.minimum(i + 1, nb - 1), 0, 0)),
            pl.BlockSpec(memory_space=pl.ANY),
            pl.BlockSpec((bm, 1), lambda i, be, nu: (i, 0)),
            pl.BlockSpec((1, D, D_FF), lambda i, be, nu: (be[i], 0, 0)),
            pl.BlockSpec((1, D, D_FF), lambda i, be, nu: (be[i], 0, 0)),
            pl.BlockSpec((1, D_FF, D), lambda i, be, nu: (be[i], 0, 0)),
        ],
        out_specs=pl.BlockSpec((bm, D), lambda i, be, nu: (i, 0)),
        scratch_shapes=[pltpu.VMEM((2, bm, D), F32), pltpu.SemaphoreType.DMA((2,))],
    )
    return pl.pallas_call(
        _expert_kernel,
        grid_spec=grid_spec,
        out_shape=jax.ShapeDtypeStruct((P, D), F32),
        compiler_params=_cparams(("arbitrary",)),
        name="experts",
    )(block_expert, n_used, tok3, tok3, h2, w_disp, wg, wu, wd)


def _combine_kernel(pos_cur_ref, pos_next_ref, y_hbm, xb_ref, gate_ref, o_ref, ybuf, sem):
    i = pl.program_id(0)
    n = pl.num_programs(0)
    slot = i % 2
    rows = ybuf.shape[1]
    tm = rows // TOP_K

    def issue(pos_ref, dst_slot):
        def body(r, _):
            _row_gather_copy(y_hbm, pos_ref[0, 0, r], ybuf, dst_slot, r, sem).start()
            return 0
        lax.fori_loop(0, rows, body, 0)

    @pl.when(i == 0)
    def _():
        issue(pos_cur_ref, 0)

    @pl.when(i + 1 < n)
    def _():
        issue(pos_next_ref, 1 - slot)

    def wait_body(r, _):
        _row_gather_copy(y_hbm, 0, ybuf, slot, r, sem).wait()
        return 0
    lax.fori_loop(0, rows, wait_body, 0)
    o_ref[...] = xb_ref[...] + gate_ref[0] * (ybuf[slot, 0:tm, :] + ybuf[slot, tm:rows, :])


def _combine(pos_tiles, y_disp, xb, gate, tiles_per_seq):
    T, D = xb.shape
    tm = COMB_TM
    n = T // tm
    smem_blk = lambda f: pl.BlockSpec((1, 1, TOP_K * tm), f, memory_space=pltpu.SMEM)
    return pl.pallas_call(
        _combine_kernel,
        grid=(n,),
        in_specs=[
            smem_blk(lambda i: (i, 0, 0)),
            smem_blk(lambda i: (jnp.minimum(i + 1, n - 1), 0, 0)),
            pl.BlockSpec(memory_space=pl.ANY),
            pl.BlockSpec((tm, D), lambda i: (i, 0)),
            pl.BlockSpec((1, 1, D), lambda i: (i // tiles_per_seq, 0, 0)),
        ],
        out_specs=pl.BlockSpec((tm, D), lambda i: (i, 0)),
        out_shape=jax.ShapeDtypeStruct((T, D), F32),
        scratch_shapes=[pltpu.VMEM((2, TOP_K * tm, D), F32), pltpu.SemaphoreType.DMA((2,))],
        compiler_params=_cparams(("arbitrary",)),
        name="combine",
    )(pos_tiles, pos_tiles, y_disp, xb, gate)


def _dispatch_plan(idx, wts):
    T = idx.shape[1]
    A = T * TOP_K
    bm = MOE_BM
    P = A + N_EXPERTS * bm
    nb = P // bm
    e_flat = idx.reshape(A)
    onehot = (e_flat[:, None] == jnp.arange(N_EXPERTS, dtype=jnp.int32)[None, :]).astype(jnp.int32)
    csum = jnp.cumsum(onehot, axis=0)
    counts = csum[-1]
    rank = jnp.take_along_axis(csum, e_flat[:, None], axis=1)[:, 0] - 1
    padded = ((counts + bm - 1) // bm) * bm
    pad_end = jnp.cumsum(padded)
    pad_start = pad_end - padded
    dest = pad_start[e_flat] + rank
    tok_of_a = jnp.tile(jnp.arange(T, dtype=jnp.int32), TOP_K)
    tok_disp = jnp.zeros((P,), jnp.int32).at[dest].set(tok_of_a)
    w_disp = jnp.zeros((P,), F32).at[dest].set(wts.reshape(A))
    block_expert = jnp.minimum(
        jnp.searchsorted(pad_end, jnp.arange(nb, dtype=jnp.int32) * bm, side='right'), N_EXPERTS - 1
    ).astype(jnp.int32)
    n_used = (pad_end[-1] // bm).astype(jnp.int32).reshape(1)
    tm = COMB_TM
    pos_tiles = dest.reshape(TOP_K, T // tm, tm).transpose(1, 0, 2).reshape(T // tm, 1, TOP_K * tm)
    return block_expert, n_used, tok_disp, w_disp.reshape(P, 1), pos_tiles.astype(jnp.int32)


def _axial_angles(S, hd):
    rows = S // GRID_W
    row = jnp.repeat(jnp.arange(rows), GRID_W).astype(F32)
    col = jnp.tile(jnp.arange(GRID_W), rows).astype(F32)
    n = hd // 4
    inv = ROPE_THETA ** (-jnp.arange(n, dtype=F32) / n)
    ang = jnp.concatenate([row[:, None] * inv, col[:, None] * inv], axis=-1)
    return jnp.cos(ang), jnp.sin(ang)


def _block_diag_gates(gate_w):
    per = LANES // RG_BW
    n_lg = D_RNN // LANES
    g = gate_w.reshape(2, 2, n_lg, per, RG_BW, RG_BW)
    eye = jnp.eye(per, dtype=gate_w.dtype)
    bd = jnp.einsum('dglpij,pq->dglpiqj', g, eye)
    return bd.reshape(2, 2, n_lg, LANES, LANES).astype(BF16)


def kernel(x_prompt, x_sample, c_prompt, c_sample, norm_mix, norm_ffn, w_ada, b_ada, w_in_ab, conv_w,
           conv_b, rg_gate_w, rg_gate_b, rg_lambda, q_norm, k_norm, w_out_ab, w_in_ret, w_out_ret,
           router_w, router_bias, exp_w_gate, exp_w_up, exp_w_down, sh_w_gate, sh_w_up, sh_w_down):
    assert x_prompt.shape[1:] == x_sample.shape[1:]
    Bp_, S, D = x_prompt.shape
    B = Bp_ + x_sample.shape[0]
    T = B * S
    x = jnp.concatenate([x_prompt, x_sample], axis=0)
    c = jnp.concatenate([c_prompt, c_sample], axis=0)
    c_rows = -(-B // 16) * 16
    c_pad = jnp.zeros((c_rows, D), F32).at[:B].set(c)
    mod = _ada_mod(c_pad, w_ada, b_ada)[:, :B].reshape(DEPTH, B, 6, 1, D)

    cos_a, sin_a = _axial_angles(S, HEAD_DIM)
    cos_att = jnp.tile(jnp.concatenate([cos_a, cos_a], axis=-1), (1, ATT_HEADS))
    sin_att = jnp.tile(jnp.concatenate([-sin_a, sin_a], axis=-1), (1, ATT_HEADS))
    cos_r, sin_r = _axial_angles(S, RET_QK)
    seg = jnp.arange(D_ATT) // HEAD_DIM
    ones_bd = (seg[:, None] == seg[None, :]).astype(BF16)
    tabs_f = _shape_tables(*_retention_tables(0.0, False), flip=False)
    tabs_b = _shape_tables(*_retention_tables(0.5, True), flip=True)
    rwt = router_w.T.astype(BF16)
    rb = router_bias.reshape(N_EXPERTS, 1).astype(F32)

    for l in range(DEPTH):
        shift1, scale1, gate1, shift2, scale2, gate2 = (mod[l, :, j] for j in range(6))
        g_mix = norm_mix[l].reshape(1, D)
        if l % 2 == 0:
            e = l // 2
            rg, q, k, v = _inproj_even(
                x, g_mix, scale1, shift1, w_in_ab[e].astype(BF16), ones_bd,
                jnp.tile(q_norm[e], ATT_HEADS).reshape(1, D_ATT),
                jnp.tile(k_norm[e], KV_HEADS).reshape(1, KV_HEADS * HEAD_DIM), cos_att, sin_att)
            y_rg = _rglru(rg, conv_w[e], conv_b[e].reshape(1, D_RNN), _block_diag_gates(rg_gate_w[e]),
                          rg_gate_b[e].reshape(2, 2, 1, D_RNN), rg_lambda[e].reshape(2, 1, D_RNN))
            y_att = _attention(q, jnp.swapaxes(k, 1, 2), v)
            x = _outproj([y_rg, y_att], w_out_ab[e].astype(BF16), x, gate1)
        else:
            o = l // 2
            q, k, v, gate = _inproj_ret(x, g_mix, scale1, shift1, w_in_ret[o].astype(BF16), cos_r, sin_r)
            kt = k.reshape(B, S, RET_HEADS, RET_QK).transpose(0, 2, 3, 1)
            y_f = _retention_pass(q, kt, v, tabs_f, reverse=False)
            y_n = _retention_pass(q, kt, v, tabs_b, reverse=True, yf=y_f, gate=gate)
            x = _outproj([y_n], w_out_ret[o].astype(BF16), x, gate1)

        h2, xb, idx, wts = _moe_pre(
            x, norm_ffn[l].reshape(1, D), scale2, shift2, gate2, rwt, rb,
            sh_w_gate[l].astype(BF16), sh_w_up[l].astype(BF16), sh_w_down[l].astype(BF16))
        block_expert, n_used, tok_disp, w_disp, pos_tiles = _dispatch_plan(idx, wts)
        y_disp = _experts(block_expert, n_used, tok_disp, h2.reshape(T, D), w_disp,
                          exp_w_gate[l].astype(BF16), exp_w_up[l].astype(BF16), exp_w_down[l].astype(BF16))
        x = _combine(pos_tiles, y_disp, xb.reshape(T, D), gate2, S // COMB_TM).reshape(B, S, D)

    return x[:Bp_], x[Bp_:]
```

```python
import functools

import jax
import jax.numpy as jnp
from jax import lax
from jax.experimental import pallas as pl
from jax.experimental.pallas import tpu as pltpu

F32 = jnp.float32
BF16 = jnp.bfloat16

D_MODEL = 1024
DEPTH = 4
GRID_W = 64
NORM_EPS = 1e-6
ROPE_THETA = 10000.0
D_RNN = 512
RG_BW = 64
CONV_W = 4
RG_C = 8.0
ATT_HEADS = 8
KV_HEADS = 2
HEAD_DIM = 64
D_ATT = ATT_HEADS * HEAD_DIM
RET_HEADS = 4
RET_QK = 256
RET_V = 512
RET_CHUNK = 128
N_EXPERTS = 32
N_GROUPS = 4
EPG = N_EXPERTS // N_GROUPS
TOP_K = 2
D_FF = 512

LANES = 128
SUBLANES = 8
VMEM_LIMIT = 52 * 1024 * 1024

ROW_TILE = 256
ATT_TQ = 128
ATT_TK = 1024
SCAN_ROWS = 128
RET_STEP_ROWS = 1024
MOE_BM = 512
COMB_TM = 256
NEG_BIG = -1e30
ATT_Q_SCALE = HEAD_DIM ** -0.5 * 1.4426950408889634


def _cparams(sem):
    return pltpu.CompilerParams(dimension_semantics=sem, vmem_limit_bytes=VMEM_LIMIT)


def _dot(a, b):
    return jnp.dot(a, b, preferred_element_type=F32)


def _silu(x):
    return x * jax.nn.sigmoid(x)


def _norm_mod(x, g, scale, shift):
    ms = jnp.mean(x * x, axis=-1, keepdims=True)
    h = x * lax.rsqrt(ms + NORM_EPS) * g
    return h * (1.0 + scale) + shift


def _ada_kernel(c_ref, w_ref, b_ref, o_ref):
    c = c_ref[...]
    o_ref[0] = _dot(_silu(c).astype(BF16), w_ref[0].astype(BF16)) + b_ref[0]


def _ada_mod(c_pad, w_ada, b_ada):
    L, D, N = w_ada.shape
    Bp = c_pad.shape[0]
    tn = D_MODEL
    return pl.pallas_call(
        _ada_kernel,
        grid=(L, N // tn),
        in_specs=[
            pl.BlockSpec((Bp, D), lambda l, j: (0, 0)),
            pl.BlockSpec((1, D, tn), lambda l, j: (l, 0, j)),
            pl.BlockSpec((1, 1, tn), lambda l, j: (l, 0, j)),
        ],
        out_specs=pl.BlockSpec((1, Bp, tn), lambda l, j: (l, 0, j)),
        out_shape=jax.ShapeDtypeStruct((L, Bp, N), F32),
        compiler_params=_cparams(("arbitrary", "arbitrary")),
        name="ada",
    )(c_pad, w_ada, b_ada.reshape(L, 1, N))


def _segment_mean_sq(x, ones_bd, width):
    sq = x * x
    hi = sq.astype(BF16)
    lo = (sq - hi.astype(F32)).astype(BF16)
    return (_dot(hi, ones_bd) + _dot(lo, ones_bd)) * (1.0 / width)


def _rope_halves(x, cos2, sin_signed, half):
    w = x.shape[-1]
    lane = lax.broadcasted_iota(jnp.int32, x.shape, 1)
    first = (lane % (2 * half)) < half
    rot = jnp.where(first, pltpu.roll(x, w - half, 1), pltpu.roll(x, half, 1))
    return x * cos2 + rot * sin_signed


def _inproj_even_kernel(x_ref, g_ref, sc_ref, sh_ref, w_ref, ones_ref, qn_ref, kn_ref,
                        cos_ref, sin_ref, rg_ref, q_ref, k_ref, v_ref):
    h = _norm_mod(x_ref[0], g_ref[...], sc_ref[0], sh_ref[0]).astype(BF16)
    rg_ref[0] = _dot(h, w_ref[:, 0:2 * D_RNN])
    c0 = 2 * D_RNN
    q = _dot(h, w_ref[:, c0:c0 + D_ATT])
    c1 = c0 + D_ATT
    kw = KV_HEADS * HEAD_DIM
    k = _dot(h, w_ref[:, c1:c1 + kw])
    v_ref[0] = _dot(h, w_ref[:, c1 + kw:c1 + 2 * kw]).astype(BF16)

    half = HEAD_DIM // 2
    ones = ones_ref[...]
    qn = q * lax.rsqrt(_segment_mean_sq(q, ones, HEAD_DIM) + NORM_EPS) * qn_ref[...]
    qr = _rope_halves(qn, cos_ref[...], sin_ref[...], half)
    q_ref[0] = (qr * ATT_Q_SCALE).astype(BF16)
    kn = k * lax.rsqrt(_segment_mean_sq(k, ones[:kw, :kw], HEAD_DIM) + NORM_EPS) * kn_ref[...]
    kr = _rope_halves(kn, cos_ref[:, :kw], sin_ref[:, :kw], half)
    k_ref[0] = kr.astype(BF16)


def _inproj_even(x, g, scale, shift, w, ones_bd, qn, kn, cos_t, sin_t):
    B, S, D = x.shape
    tm = ROW_TILE
    N = w.shape[1]
    kw = KV_HEADS * HEAD_DIM
    row = lambda b, i: (b, i, 0)
    per_b = lambda b, i: (b, 0, 0)
    const2 = lambda b, i: (0, 0)
    return pl.pallas_call(
        _inproj_even_kernel,
        grid=(B, S // tm),
        in_specs=[
            pl.BlockSpec((1, tm, D), row),
            pl.BlockSpec((1, D), const2),
            pl.BlockSpec((1, 1, D), per_b),
            pl.BlockSpec((1, 1, D), per_b),
            pl.BlockSpec((D, N), const2),
            pl.BlockSpec((D_ATT, D_ATT), const2),
            pl.BlockSpec((1, D_ATT), const2),
            pl.BlockSpec((1, kw), const2),
            pl.BlockSpec((tm, D_ATT), lambda b, i: (i, 0)),
            pl.BlockSpec((tm, D_ATT), lambda b, i: (i, 0)),
        ],
        out_specs=[
            pl.BlockSpec((1, tm, 2 * D_RNN), row),
            pl.BlockSpec((1, tm, D_ATT), row),
            pl.BlockSpec((1, tm, kw), row),
            pl.BlockSpec((1, tm, kw), row),
        ],
        out_shape=[
            jax.ShapeDtypeStruct((B, S, 2 * D_RNN), F32),
            jax.ShapeDtypeStruct((B, S, D_ATT), BF16),
            jax.ShapeDtypeStruct((B, S, kw), BF16),
            jax.ShapeDtypeStruct((B, S, kw), BF16),
        ],
        compiler_params=_cparams(("arbitrary", "arbitrary")),
        name="inproj_even",
    )(x, g, scale, shift, w, ones_bd, qn, kn, cos_t, sin_t)


def _scan_chunk(a, b, reverse):
    R = a.shape[0]
    row = lax.broadcasted_iota(jnp.int32, a.shape, 0)
    s = 1
    while s < R:
        if reverse:
            keep = row < R - s
            shift = R - s
        else:
            keep = row >= s
            shift = s
        a_sh = jnp.where(keep, pltpu.roll(a, shift, 0), 1.0)
        b_sh = jnp.where(keep, pltpu.roll(b, shift, 0), 0.0)
        b = a * b_sh + b
        a = a * a_sh
        s *= 2
    return a, b


def _rglru_kernel(x_ref, gate_ref, cw_ref, cb_ref, gw_ref, gb_ref, lam_ref, y_ref, hf_ref, carry_ref):
    S = x_ref.shape[1]
    R = SCAN_ROWS
    n_chunks = S // R
    cw = cw_ref[...]
    cb = cb_ref[...]

    def conv_chunk(c):
        r0 = pl.multiple_of(c * R, R)
        main = x_ref[0, pl.ds(r0, R), :]
        lo = pl.multiple_of(jnp.maximum(r0 - SUBLANES, 0), SUBLANES)
        hi = pl.multiple_of(jnp.minimum(r0 + R, S - SUBLANES), SUBLANES)
        prev = jnp.where(c > 0, x_ref[0, pl.ds(lo, SUBLANES), :], 0.0)
        nxt = jnp.where(c < n_chunks - 1, x_ref[0, pl.ds(hi, SUBLANES), :], 0.0)
        ext = jnp.concatenate([prev, main, nxt], axis=0)
        xc = cb
        for tap in range(CONV_W):
            o = SUBLANES - 2 + tap
            xc = xc + ext[o:o + R, :] * cw[tap:tap + 1, :]
        return r0, xc

    def direction(c, d, reverse):
        r0, xc = conv_chunk(c)
        xb = xc.astype(BF16)
        r = jax.nn.sigmoid(_dot(xb, gw_ref[d, 0, 0]) + gb_ref[d, 0])
        i = jax.nn.sigmoid(_dot(xb, gw_ref[d, 1, 0]) + gb_ref[d, 1])
        log_a = -RG_C * r * jax.nn.softplus(-lam_ref[d])
        a = jnp.exp(log_a)
        b = jnp.sqrt(jnp.tanh(-log_a) * (a * a + 1.0)) * (i * xc)
        a_cum, h0 = _scan_chunk(a, b, reverse)
        h = a_cum * carry_ref[...] + h0
        carry_ref[...] = h[0:1, :] if reverse else h[R - 1:R, :]
        return r0, h

    carry_ref[...] = jnp.zeros_like(carry_ref)

    def fwd_body(c, _):
        r0, h = direction(c, 0, False)
        hf_ref[pl.ds(r0, R), :] = h
        return 0

    lax.fori_loop(0, n_chunks, fwd_body, 0)
    carry_ref[...] = jnp.zeros_like(carry_ref)

    def bwd_body(j, _):
        c = n_chunks - 1 - j
        r0, h = direction(c, 1, True)
        gate = gate_ref[0, pl.ds(r0, R), :]
        y = jax.nn.gelu(gate) * (hf_ref[pl.ds(r0, R), :] + h)
        y_ref[0, pl.ds(r0, R), :] = y.astype(y_ref.dtype)
        return 0

    lax.fori_loop(0, n_chunks, bwd_body, 0)


def _rglru(rg, conv_w, conv_b, gw_bd, gate_b, lam):
    B, S, _ = rg.shape
    n_lg = D_RNN // LANES
    return pl.pallas_call(
        _rglru_kernel,
        grid=(B, n_lg),
        in_specs=[
            pl.BlockSpec((1, S, LANES), lambda b, j: (b, 0, n_lg + j)),
            pl.BlockSpec((1, S, LANES), lambda b, j: (b, 0, j)),
            pl.BlockSpec((CONV_W, LANES), lambda b, j: (0, j)),
            pl.BlockSpec((1, LANES), lambda b, j: (0, j)),
            pl.BlockSpec((2, 2, 1, LANES, LANES), lambda b, j: (0, 0, j, 0, 0)),
            pl.BlockSpec((2, 2, 1, LANES), lambda b, j: (0, 0, 0, j)),
            pl.BlockSpec((2, 1, LANES), lambda b, j: (0, 0, j)),
        ],
        out_specs=pl.BlockSpec((1, S, LANES), lambda b, j: (b, 0, j)),
        out_shape=jax.ShapeDtypeStruct((B, S, D_RNN), BF16),
        scratch_shapes=[pltpu.VMEM((S, LANES), F32), pltpu.VMEM((1, LANES), F32)],
        compiler_params=_cparams(("arbitrary", "arbitrary")),
        name="rglru",
    )(rg, rg, conv_w, conv_b, gw_bd, gate_b, lam)


def _attn_kernel(q_ref, kt_ref, v_ref, o_ref, m_ref, acc_ref):
    S = kt_ref.shape[3]
    G = ATT_HEADS // KV_HEADS
    reps = ATT_TK // LANES
    qs = [q_ref[0, :, g * HEAD_DIM:(g + 1) * HEAD_DIM] for g in range(G)]
    m_ref[...] = jnp.full_like(m_ref, NEG_BIG)
    acc_ref[...] = jnp.zeros_like(acc_ref)

    def scores(t):
        j, g = divmod(t, G)
        return _dot(qs[g], kt_ref[0, 0, :, j * ATT_TK:(j + 1) * ATT_TK])

    n_items = (S // ATT_TK) * G
    s_next = scores(0)
    for t in range(n_items):
        j, g = divmod(t, G)
        s = s_next
        if t + 1 < n_items:
            s_next = scores(t + 1)
        m_old = m_ref[g]
        m_new = jnp.maximum(m_old, jnp.max(s, axis=1, keepdims=True))
        p = jnp.exp2(s - jnp.concatenate([m_new] * reps, axis=1))
        pv = _dot(p.astype(BF16), v_ref[0, 0, j * ATT_TK:(j + 1) * ATT_TK, :])
        acc_ref[g] = jnp.exp2(m_old - m_new) * acc_ref[g] + pv
        m_ref[g] = m_new
    outs = []
    for g in range(G):
        acc = acc_ref[g]
        outs.append(acc[:, :HEAD_DIM] / acc[:, HEAD_DIM:2 * HEAD_DIM])
    o_ref[0] = jnp.concatenate(outs, axis=1).astype(o_ref.dtype)


def _attention(q, kt, v_ext):
    B, S, _ = q.shape
    G = ATT_HEADS // KV_HEADS
    tq = ATT_TQ
    qw = G * HEAD_DIM
    return pl.pallas_call(
        _attn_kernel,
        grid=(B, KV_HEADS, S // tq),
        in_specs=[
            pl.BlockSpec((1, tq, qw), lambda b, h, i: (b, i, h)),
            pl.BlockSpec((1, 1, HEAD_DIM, S), lambda b, h, i: (b, h, 0, 0)),
            pl.BlockSpec((1, 1, S, LANES), lambda b, h, i: (b, h, 0, 0)),
        ],
        out_specs=pl.BlockSpec((1, tq, qw), lambda b, h, i: (b, i, h)),
        out_shape=jax.ShapeDtypeStruct((B, S, D_ATT), BF16),
        scratch_shapes=[
            pltpu.VMEM((G, tq, LANES), F32),
            pltpu.VMEM((G, tq, LANES), F32),
        ],
        compiler_params=_cparams(("arbitrary", "arbitrary", "arbitrary")),
        name="attn",
    )(q, kt, v_ext)


def _attn_kv_layout(k, v):
    B, S, _ = k.shape
    kt = k.reshape(B, S, KV_HEADS, HEAD_DIM).transpose(0, 2, 3, 1)
    vh = v.reshape(B, S, KV_HEADS, HEAD_DIM).transpose(0, 2, 1, 3)
    ones = jnp.ones((B, KV_HEADS, S, LANES - HEAD_DIM), v.dtype)
    return kt, jnp.concatenate([vh, ones], axis=-1)


def _outproj_kernel(*refs, n_in):
    y_refs = refs[:n_in]
    w_ref, x_ref, gate_ref, o_ref = refs[n_in:]
    acc = None
    off = 0
    for yr in y_refs:
        kd = yr.shape[-1]
        part = _dot(yr[0], w_ref[off:off + kd, :])
        acc = part if acc is None else acc + part
        off += kd
    o_ref[0] = x_ref[0] + gate_ref[0] * acc


def _outproj(ys, w, x, gate):
    B, S, D = x.shape
    tm = ROW_TILE
    row = lambda b, i: (b, i, 0)
    in_specs = [pl.BlockSpec((1, tm, y.shape[-1]), row) for y in ys]
    in_specs += [
        pl.BlockSpec(w.shape, lambda b, i: (0, 0)),
        pl.BlockSpec((1, tm, D), row),
        pl.BlockSpec((1, 1, D), lambda b, i: (b, 0, 0)),
    ]
    return pl.pallas_call(
        functools.partial(_outproj_kernel, n_in=len(ys)),
        grid=(B, S // tm),
        in_specs=in_specs,
        out_specs=pl.BlockSpec((1, tm, D), row),
        out_shape=jax.ShapeDtypeStruct((B, S, D), F32),
        compiler_params=_cparams(("arbitrary", "arbitrary")),
        name="outproj",
    )(*ys, w, x, gate)


def _inproj_ret_kernel(x_ref, g_ref, sc_ref, sh_ref, w_ref, cos_ref, sin_ref,
                       q_ref, k_ref, v_ref, gate_ref):
    h = _norm_mod(x_ref[0], g_ref[...], sc_ref[0], sh_ref[0]).astype(BF16)
    cos = cos_ref[...]
    sin = sin_ref[...]
    half = RET_QK // 2
    dqk = RET_HEADS * RET_QK
    dv = RET_HEADS * RET_V

    def rope(t):
        t1 = t[:, :half]
        t2 = t[:, half:]
        return jnp.concatenate([t1 * cos - t2 * sin, t2 * cos + t1 * sin], axis=-1)

    for hd in range(RET_HEADS):
        c = hd * RET_QK
        q_ref[0, :, c:c + RET_QK] = rope(_dot(h, w_ref[:, c:c + RET_QK])).astype(BF16)
        kh = rope(_dot(h, w_ref[:, dqk + c:dqk + c + RET_QK])) * (RET_QK ** -0.5)
        k_ref[0, :, c:c + RET_QK] = kh.astype(BF16)
    for hd in range(RET_HEADS):
        c = hd * RET_V
        v_ref[0, :, c:c + RET_V] = _dot(h, w_ref[:, 2 * dqk + c:2 * dqk + c + RET_V]).astype(BF16)
        gate_ref[0, :, c:c + RET_V] = _dot(h, w_ref[:, 2 * dqk + dv + c:2 * dqk + dv + c + RET_V])


def _inproj_ret(x, g, scale, shift, w, cos_t, sin_t):
    B, S, D = x.shape
    tm = ROW_TILE
    N = w.shape[1]
    dqk = RET_HEADS * RET_QK
    dv = RET_HEADS * RET_V
    half = RET_QK // 2
    row = lambda b, i: (b, i, 0)
    per_b = lambda b, i: (b, 0, 0)
    const2 = lambda b, i: (0, 0)
    return pl.pallas_call(
        _inproj_ret_kernel,
        grid=(B, S // tm),
        in_specs=[
            pl.BlockSpec((1, tm, D), row),
            pl.BlockSpec((1, D), const2),
            pl.BlockSpec((1, 1, D), per_b),
            pl.BlockSpec((1, 1, D), per_b),
            pl.BlockSpec((D, N), const2),
            pl.BlockSpec((tm, half), lambda b, i: (i, 0)),
            pl.BlockSpec((tm, half), lambda b, i: (i, 0)),
        ],
        out_specs=[
            pl.BlockSpec((1, tm, dqk), row),
            pl.BlockSpec((1, tm, dqk), row),
            pl.BlockSpec((1, tm, dv), row),
            pl.BlockSpec((1, tm, dv), row),
        ],
        out_shape=[
            jax.ShapeDtypeStruct((B, S, dqk), BF16),
            jax.ShapeDtypeStruct((B, S, dqk), BF16),
            jax.ShapeDtypeStruct((B, S, dv), BF16),
            jax.ShapeDtypeStruct((B, S, dv), F32),
        ],
        compiler_params=_cparams(("arbitrary", "arbitrary")),
        name="inproj_ret",
    )(x, g, scale, shift, w, cos_t, sin_t)


def _ret_kernel(*refs, reverse):
    if reverse:
        q_ref, kt_ref, v_ref, dm_ref, qd_ref, kd_ref, cd_ref, yf_ref, g_ref, o_ref, st_ref = refs
    else:
        q_ref, kt_ref, v_ref, dm_ref, qd_ref, kd_ref, cd_ref, o_ref, st_ref = refs
    C = RET_CHUNK
    n_sub = q_ref.shape[1] // C

    @pl.when(pl.program_id(2) == 0)
    def _():
        st_ref[...] = jnp.zeros_like(st_ref)

    dm = dm_ref[0]
    qd = qd_ref[0]
    kd = kd_ref[0]
    cd = cd_ref[0]
    order = range(n_sub - 1, -1, -1) if reverse else range(n_sub)
    for ci in order:
        r = slice(ci * C, (ci + 1) * C)
        qc = q_ref[0, r, :]
        ktc = kt_ref[0, 0, :, r]
        vc = v_ref[0, r, :]
        st = st_ref[...]
        scores = _dot(qc, ktc) * dm
        out = _dot(scores.astype(BF16), vc) + _dot(qc, st.astype(BF16)) * qd
        st_ref[...] = st * cd + _dot((ktc.astype(F32) * kd).astype(BF16), vc)
        if reverse:
            y = yf_ref[0, r, :] + out
            y = y * lax.rsqrt(jnp.mean(y * y, axis=-1, keepdims=True) + NORM_EPS)
            o_ref[0, r, :] = (y * _silu(g_ref[0, r, :])).astype(o_ref.dtype)
        else:
            o_ref[0, r, :] = out


def _retention_pass(q, kt, v, tabs, reverse, yf=None, gate=None):
    B, S, _ = q.shape
    ts = min(RET_STEP_ROWS, S)
    n = S // ts
    C = RET_CHUNK
    step = (lambda i: n - 1 - i) if reverse else (lambda i: i)
    rows = lambda width: pl.BlockSpec((1, ts, width), lambda b, h, i: (b, step(i), h))
    per_h = lambda shape: pl.BlockSpec((1,) + shape, lambda b, h, i: (h, 0, 0))
    in_specs = [
        rows(RET_QK),
        pl.BlockSpec((1, 1, RET_QK, ts), lambda b, h, i: (b, h, 0, step(i))),
        rows(RET_V),
        per_h((C, C)), per_h((C, 1)), per_h((1, C)), per_h((1, 1)),
    ]
    args = [q, kt, v, *tabs]
    if reverse:
        in_specs += [rows(RET_V), rows(RET_V)]
        args += [yf, gate]
    return pl.pallas_call(
        functools.partial(_ret_kernel, reverse=reverse),
        grid=(B, RET_HEADS, n),
        in_specs=in_specs,
        out_specs=rows(RET_V),
        out_shape=jax.ShapeDtypeStruct((B, S, RET_HEADS * RET_V), BF16 if reverse else F32),
        scratch_shapes=[pltpu.VMEM((RET_QK, RET_V), F32)],
        compiler_params=_cparams(("arbitrary", "arbitrary", "arbitrary")),
        name="retention_bwd" if reverse else "retention_fwd",
    )(*args)


def _retention_tables(offset, strict):
    C = RET_CHUNK
    log_gamma = jnp.log1p(-jnp.exp2(-5.0 - offset - jnp.arange(RET_HEADS, dtype=F32)))
    idx = jnp.arange(C, dtype=F32)
    diff = idx[:, None] - idx[None, :]
    mask = (diff > 0) if strict else (diff >= 0)
    d_intra = jnp.where(mask, jnp.exp(jnp.where(mask, diff, 0.0)[None] * log_gamma[:, None, None]), 0.0)
    q_dec = jnp.exp((idx + 1.0)[None, :] * log_gamma[:, None])
    k_dec = jnp.exp((C - 1.0 - idx)[None, :] * log_gamma[:, None])
    chunk_dec = jnp.exp(C * log_gamma)
    return d_intra, q_dec, k_dec, chunk_dec


def _shape_tables(d_intra, q_dec, k_dec, chunk_dec, flip):
    if flip:
        d_intra = d_intra[:, ::-1, ::-1]
        q_dec = q_dec[:, ::-1]
        k_dec = k_dec[:, ::-1]
    return (d_intra, q_dec[:, :, None], k_dec[:, None, :], chunk_dec[:, None, None])


def _first_argmax(x, iota, size):
    m = jnp.max(x, axis=0, keepdims=True)
    return m, jnp.min(jnp.where(x == m, iota, size), axis=0, keepdims=True)


def _route_kernel(x_ref, g_ref, sc_ref, sh_ref, rwt_ref, rb_ref, idx_ref, wt_ref):
    hb = _norm_mod(x_ref[0], g_ref[...], sc_ref[0], sh_ref[0]).astype(BF16)
    tm = hb.shape[0]
    logits = lax.dot_general(rwt_ref[...], hb, (((1,), (1,)), ((), ())), preferred_element_type=F32)
    scores = jax.nn.sigmoid(logits)
    biased = scores + rb_ref[...]
    iota_e = lax.broadcasted_iota(jnp.int32, (EPG, tm), 0).astype(F32)
    group_scores = []
    for gi in range(N_GROUPS):
        bg = biased[gi * EPG:(gi + 1) * EPG, :]
        m1, i1 = _first_argmax(bg, iota_e, EPG)
        m2 = jnp.max(jnp.where(iota_e == i1, -jnp.inf, bg), axis=0, keepdims=True)
        group_scores.append(m1 + m2)
    gs = jnp.concatenate(group_scores, axis=0)
    iota_g = lax.broadcasted_iota(jnp.int32, (N_GROUPS, tm), 0).astype(F32)
    _, g_sel = _first_argmax(gs, iota_g, N_GROUPS)
    in_b = jnp.zeros((EPG, tm), F32)
    in_s = jnp.zeros((EPG, tm), F32)
    for gi in range(N_GROUPS):
        sel = g_sel == gi
        in_b = jnp.where(sel, biased[gi * EPG:(gi + 1) * EPG, :], in_b)
        in_s = jnp.where(sel, scores[gi * EPG:(gi + 1) * EPG, :], in_s)
    _, l1 = _first_argmax(in_b, iota_e, EPG)
    _, l2 = _first_argmax(jnp.where(iota_e == l1, -jnp.inf, in_b), iota_e, EPG)
    w1 = jnp.sum(jnp.where(iota_e == l1, in_s, 0.0), axis=0, keepdims=True)
    w2 = jnp.sum(jnp.where(iota_e == l2, in_s, 0.0), axis=0, keepdims=True)
    wsum = w1 + w2
    idx_ref[...] = jnp.concatenate([g_sel * EPG + l1, g_sel * EPG + l2], axis=0).astype(jnp.int32)
    wt_ref[...] = jnp.concatenate([w1 / wsum, w2 / wsum], axis=0)


def _moe_route(x, g, scale, shift, rwt, rb):
    B, S, D = x.shape
    tm = ROW_TILE
    n = S // tm
    per_b = lambda b, i: (b, 0, 0)
    const2 = lambda b, i: (0, 0)
    tok = lambda b, i: (0, b * n + i)
    return pl.pallas_call(
        _route_kernel,
        grid=(B, n),
        in_specs=[
            pl.BlockSpec((1, tm, D), lambda b, i: (b, i, 0)),
            pl.BlockSpec((1, D), const2),
            pl.BlockSpec((1, 1, D), per_b),
            pl.BlockSpec((1, 1, D), per_b),
            pl.BlockSpec((N_EXPERTS, D), const2),
            pl.BlockSpec((N_EXPERTS, 1), const2),
        ],
        out_specs=[pl.BlockSpec((TOP_K, tm), tok), pl.BlockSpec((TOP_K, tm), tok)],
        out_shape=[
            jax.ShapeDtypeStruct((TOP_K, B * S), jnp.int32),
            jax.ShapeDtypeStruct((TOP_K, B * S), F32),
        ],
        compiler_params=_cparams(("arbitrary", "arbitrary")),
        name="moe_route",
    )(x, g, scale, shift, rwt, rb)


def _dispatch_plan(idx):
    T = idx.shape[1]
    A = T * TOP_K
    bm = MOE_BM
    nb = (A + N_EXPERTS * bm) // bm
    e_flat = idx.reshape(A)
    onehot = (e_flat[:, None] == jnp.arange(N_EXPERTS, dtype=jnp.int32)[None, :]).astype(jnp.int32)
    csum = jnp.cumsum(onehot, axis=0)
    counts = csum[-1]
    rank = jnp.take_along_axis(csum, e_flat[:, None], axis=1)[:, 0] - 1
    padded = ((counts + bm - 1) // bm) * bm
    pad_end = jnp.cumsum(padded)
    pad_start = pad_end - padded
    dest = (pad_start[e_flat] + rank).astype(jnp.int32)
    block_expert = jnp.minimum(
        jnp.searchsorted(pad_end, jnp.arange(nb, dtype=jnp.int32) * bm, side='right'), N_EXPERTS - 1
    ).astype(jnp.int32)
    n_used = (pad_end[-1] // bm).astype(jnp.int32).reshape(1)
    tail_block = jnp.maximum(pad_end // bm - 1, 0).astype(jnp.int32)
    tm = COMB_TM
    dest_tiles = dest.reshape(TOP_K, T // tm, tm).transpose(1, 0, 2).reshape(T // tm, 1, TOP_K * tm)
    return block_expert, n_used, tail_block, dest_tiles


def _zero_block_kernel(tail_ref, o_ref):
    del tail_ref
    o_ref[...] = jnp.zeros_like(o_ref)


def _zero_tail_blocks(tail_block, n_rows, width):
    grid_spec = pltpu.PrefetchScalarGridSpec(
        num_scalar_prefetch=1,
        grid=(N_EXPERTS,),
        in_specs=[],
        out_specs=pl.BlockSpec((MOE_BM, width), lambda e, tail: (tail[e], 0)),
    )
    return pl.pallas_call(
        _zero_block_kernel,
        grid_spec=grid_spec,
        out_shape=jax.ShapeDtypeStruct((n_rows, width), F32),
        compiler_params=_cparams(("arbitrary",)),
        name="zero_tail",
    )(tail_block)


def _row_copy(src, src_row, dst, dst_row, sem):
    return pltpu.make_async_copy(src.at[pl.ds(src_row, 1), :], dst.at[pl.ds(dst_row, 1), :], sem)


def _dispatch_kernel(dest_ref, x_ref, g_ref, sc_ref, sh_ref, gate_ref, sg_ref, su_ref, sd_ref, xd_in_ref,
                     xb_ref, xd_ref, hbuf, sem):
    del xd_in_ref
    x = x_ref[0]
    tm = x.shape[0]
    h = _norm_mod(x, g_ref[...], sc_ref[0], sh_ref[0])
    hbuf[...] = h
    for r in range(tm):
        for k in range(TOP_K):
            _row_copy(hbuf, r, xd_ref, dest_ref[0, 0, k * tm + r], sem).start()
    hb = h.astype(BF16)
    act = _silu(_dot(hb, sg_ref[...])) * _dot(hb, su_ref[...])
    xb_ref[0] = x + gate_ref[0] * _dot(act.astype(BF16), sd_ref[...])
    for _ in range(TOP_K * tm):
        _row_copy(hbuf, 0, xd_ref, 0, sem).wait()


def _dispatch_shared(dest_tiles, x, g, scale, shift, gate, sg, su, sd, x_disp):
    B, S, D = x.shape
    tm = COMB_TM
    n = S // tm
    row = lambda b, i: (b, i, 0)
    per_b = lambda b, i: (b, 0, 0)
    const2 = lambda b, i: (0, 0)
    return pl.pallas_call(
        _dispatch_kernel,
        grid=(B, n),
        in_specs=[
            pl.BlockSpec((1, 1, TOP_K * tm), lambda b, i: (b * n + i, 0, 0), memory_space=pltpu.SMEM),
            pl.BlockSpec((1, tm, D), row),
            pl.BlockSpec((1, D), const2),
            pl.BlockSpec((1, 1, D), per_b),
            pl.BlockSpec((1, 1, D), per_b),
            pl.BlockSpec((1, 1, D), per_b),
            pl.BlockSpec((D, D_FF), const2),
            pl.BlockSpec((D, D_FF), const2),
            pl.BlockSpec((D_FF, D), const2),
            pl.BlockSpec(memory_space=pl.ANY),
        ],
        out_specs=[pl.BlockSpec((1, tm, D), row), pl.BlockSpec(memory_space=pl.ANY)],
        out_shape=[jax.ShapeDtypeStruct((B, S, D), F32), jax.ShapeDtypeStruct(x_disp.shape, F32)],
        input_output_aliases={9: 1},
        scratch_shapes=[pltpu.VMEM((tm, D), F32), pltpu.SemaphoreType.DMA(())],
        compiler_params=_cparams(("arbitrary", "arbitrary")),
        name="moe_dispatch",
    )(dest_tiles, x, g, scale, shift, gate, sg, su, sd, x_disp)


def _expert_kernel(be_ref, nu_ref, x_ref, wg_ref, wu_ref, wd_ref, o_ref):
    del be_ref

    @pl.when(pl.program_id(0) < nu_ref[0])
    def _():
        xb = x_ref[...].astype(BF16)
        act = _silu(_dot(xb, wg_ref[0])) * _dot(xb, wu_ref[0])
        o_ref[...] = _dot(act.astype(BF16), wd_ref[0])

    @pl.when(pl.program_id(0) >= nu_ref[0])
    def _():
        o_ref[...] = jnp.zeros_like(o_ref)


def _experts(block_expert, n_used, x_disp, wg, wu, wd):
    P, D = x_disp.shape
    bm = MOE_BM
    nb = P // bm
    blk = lambda i, be, nu: (jnp.minimum(i, jnp.maximum(nu[0] - 1, 0)), 0)
    wsel = lambda i, be, nu: (be[jnp.minimum(i, jnp.maximum(nu[0] - 1, 0))], 0, 0)
    grid_spec = pltpu.PrefetchScalarGridSpec(
        num_scalar_prefetch=2,
        grid=(nb,),
        in_specs=[
            pl.BlockSpec((bm, D), blk),
            pl.BlockSpec((1, D, D_FF), wsel),
            pl.BlockSpec((1, D, D_FF), wsel),
            pl.BlockSpec((1, D_FF, D), wsel),
        ],
        out_specs=pl.BlockSpec((bm, D), lambda i, be, nu: (i, 0)),
    )
    return pl.pallas_call(
        _expert_kernel,
        grid_spec=grid_spec,
        out_shape=jax.ShapeDtypeStruct((P, D), F32),
        compiler_params=_cparams(("arbitrary",)),
        name="experts",
    )(block_expert, n_used, x_disp, wg, wu, wd)


def _combine_kernel(pos_cur_ref, pos_next_ref, y_hbm, w_ref, xb_ref, gate_ref, o_ref, ybuf, sem):
    i = pl.program_id(0)
    n = pl.num_programs(0)
    slot = i % 2
    rows = ybuf.shape[1]
    tm = rows // TOP_K

    def issue(pos_ref, dst_slot):
        for r in range(rows):
            _row_copy(y_hbm, pos_ref[0, 0, r], ybuf.at[dst_slot], r, sem.at[dst_slot]).start()

    @pl.when(i == 0)
    def _():
        issue(pos_cur_ref, 0)

    @pl.when(i + 1 < n)
    def _():
        issue(pos_next_ref, 1 - slot)

    for _ in range(rows):
        _row_copy(y_hbm, 0, ybuf.at[slot], 0, sem.at[slot]).wait()
    w = w_ref[...]
    y = ybuf[slot, 0:tm, :] * w[:, 0:1] + ybuf[slot, tm:rows, :] * w[:, 1:2]
    o_ref[...] = xb_ref[...] + gate_ref[0] * y


def _combine(pos_tiles, y_disp, w_cols, xb, gate, tiles_per_seq):
    T, D = xb.shape
    tm = COMB_TM
    n = T // tm
    smem_blk = lambda f: pl.BlockSpec((1, 1, TOP_K * tm), f, memory_space=pltpu.SMEM)
    return pl.pallas_call(
        _combine_kernel,
        grid=(n,),
        in_specs=[
            smem_blk(lambda i: (i, 0, 0)),
            smem_blk(lambda i: (jnp.minimum(i + 1, n - 1), 0, 0)),
            pl.BlockSpec(memory_space=pl.ANY),
            pl.BlockSpec((tm, TOP_K), lambda i: (i, 0)),
            pl.BlockSpec((tm, D), lambda i: (i, 0)),
            pl.BlockSpec((1, 1, D), lambda i: (i // tiles_per_seq, 0, 0)),
        ],
        out_specs=pl.BlockSpec((tm, D), lambda i: (i, 0)),
        out_shape=jax.ShapeDtypeStruct((T, D), F32),
        scratch_shapes=[pltpu.VMEM((2, TOP_K * tm, D), F32), pltpu.SemaphoreType.DMA((2,))],
        compiler_params=_cparams(("arbitrary",)),
        name="combine",
    )(pos_tiles, pos_tiles, y_disp, w_cols, xb, gate)


def _axial_angles(S, hd):
    rows = S // GRID_W
    row = jnp.repeat(jnp.arange(rows), GRID_W).astype(F32)
    col = jnp.tile(jnp.arange(GRID_W), rows).astype(F32)
    n = hd // 4
    inv = ROPE_THETA ** (-jnp.arange(n, dtype=F32) / n)
    ang = jnp.concatenate([row[:, None] * inv, col[:, None] * inv], axis=-1)
    return jnp.cos(ang), jnp.sin(ang)


def _block_diag_gates(gate_w):
    per = LANES // RG_BW
    n_lg = D_RNN // LANES
    g = gate_w.reshape(2, 2, n_lg, per, RG_BW, RG_BW)
    eye = jnp.eye(per, dtype=gate_w.dtype)
    bd = jnp.einsum('dglpij,pq->dglpiqj', g, eye)
    return bd.reshape(2, 2, n_lg, LANES, LANES).astype(BF16)


def kernel(x_prompt, x_sample, c_prompt, c_sample, norm_mix, norm_ffn, w_ada, b_ada, w_in_ab, conv_w,
           conv_b, rg_gate_w, rg_gate_b, rg_lambda, q_norm, k_norm, w_out_ab, w_in_ret, w_out_ret,
           router_w, router_bias, exp_w_gate, exp_w_up, exp_w_down, sh_w_gate, sh_w_up, sh_w_down):
    assert x_prompt.shape[1:] == x_sample.shape[1:]
    Bp_, S, D = x_prompt.shape
    B = Bp_ + x_sample.shape[0]
    T = B * S
    x = jnp.concatenate([x_prompt, x_sample], axis=0)
    c = jnp.concatenate([c_prompt, c_sample], axis=0)
    c_rows = -(-B // 16) * 16
    c_pad = jnp.zeros((c_rows, D), F32).at[:B].set(c)
    mod = _ada_mod(c_pad, w_ada, b_ada)[:, :B].reshape(DEPTH, B, 6, 1, D)

    cos_a, sin_a = _axial_angles(S, HEAD_DIM)
    cos_att = jnp.tile(jnp.concatenate([cos_a, cos_a], axis=-1), (1, ATT_HEADS))
    sin_att = jnp.tile(jnp.concatenate([-sin_a, sin_a], axis=-1), (1, ATT_HEADS))
    cos_r, sin_r = _axial_angles(S, RET_QK)
    seg = jnp.arange(D_ATT) // HEAD_DIM
    ones_bd = (seg[:, None] == seg[None, :]).astype(BF16)
    tabs_f = _shape_tables(*_retention_tables(0.0, False), flip=False)
    tabs_b = _shape_tables(*_retention_tables(0.5, True), flip=True)
    rwt = router_w.T.astype(BF16)
    rb = router_bias.reshape(N_EXPERTS, 1).astype(F32)

    for l in range(DEPTH):
        shift1, scale1, gate1, shift2, scale2, gate2 = (mod[l, :, j] for j in range(6))
        g_mix = norm_mix[l].reshape(1, D)
        if l % 2 == 0:
            e = l // 2
            rg, q, k, v = _inproj_even(
                x, g_mix, scale1, shift1, w_in_ab[e].astype(BF16), ones_bd,
                jnp.tile(q_norm[e], ATT_HEADS).reshape(1, D_ATT),
                jnp.tile(k_norm[e], KV_HEADS).reshape(1, KV_HEADS * HEAD_DIM), cos_att, sin_att)
            y_rg = _rglru(rg, conv_w[e], conv_b[e].reshape(1, D_RNN), _block_diag_gates(rg_gate_w[e]),
                          rg_gate_b[e].reshape(2, 2, 1, D_RNN), rg_lambda[e].reshape(2, 1, D_RNN))
            y_att = _attention(q, *_attn_kv_layout(k, v))
            x = _outproj([y_rg, y_att], w_out_ab[e].astype(BF16), x, gate1)
        else:
            o = l // 2
            q, k, v, gate = _inproj_ret(x, g_mix, scale1, shift1, w_in_ret[o].astype(BF16), cos_r, sin_r)
            kt = k.reshape(B, S, RET_HEADS, RET_QK).transpose(0, 2, 3, 1)
            y_f = _retention_pass(q, kt, v, tabs_f, reverse=False)
            y_n = _retention_pass(q, kt, v, tabs_b, reverse=True, yf=y_f, gate=gate)
            x = _outproj([y_n], w_out_ret[o].astype(BF16), x, gate1)

        g_ffn = norm_ffn[l].reshape(1, D)
        idx, wts = _moe_route(x, g_ffn, scale2, shift2, rwt, rb)
        block_expert, n_used, tail_block, dest_tiles = _dispatch_plan(idx)
        x_disp = _zero_tail_blocks(tail_block, T * TOP_K + N_EXPERTS * MOE_BM, D)
        xb, x_disp = _dispatch_shared(
            dest_tiles, x, g_ffn, scale2, shift2, gate2,
            sh_w_gate[l].astype(BF16), sh_w_up[l].astype(BF16), sh_w_down[l].astype(BF16), x_disp)
        y_disp = _experts(block_expert, n_used, x_disp,
                          exp_w_gate[l].astype(BF16), exp_w_up[l].astype(BF16), exp_w_down[l].astype(BF16))
        x = _combine(dest_tiles, y_disp, wts.T, xb.reshape(T, D), gate2, S // COMB_TM).reshape(B, S, D)

    return x[:Bp_], x[Bp_:]
```

```python
import functools

import jax
import jax.numpy as jnp
from jax import lax
from jax.experimental import pallas as pl
from jax.experimental.pallas import tpu as pltpu

F32 = jnp.float32
BF16 = jnp.bfloat16

D_MODEL = 1024
DEPTH = 4
GRID_W = 64
NORM_EPS = 1e-6
ROPE_THETA = 10000.0
D_RNN = 512
RG_BW = 64
CONV_W = 4
RG_C = 8.0
ATT_HEADS = 8
KV_HEADS = 2
HEAD_DIM = 64
D_ATT = ATT_HEADS * HEAD_DIM
RET_HEADS = 4
RET_QK = 256
RET_V = 512
RET_CHUNK = 128
N_EXPERTS = 32
N_GROUPS = 4
EPG = N_EXPERTS // N_GROUPS
TOP_K = 2
D_FF = 512

LANES = 128
SUBLANES = 8
VMEM_LIMIT = 52 * 1024 * 1024

ROW_TILE = 256
ATT_TQ = 128
ATT_TK = 1024
SCAN_ROWS = 128
RET_STEP_ROWS = 1024
MOE_BM = 512
COMB_TM = 256
NEG_BIG = -1e30
ATT_Q_SCALE = HEAD_DIM ** -0.5 * 1.4426950408889634


def _cparams(sem):
    return pltpu.CompilerParams(dimension_semantics=sem, vmem_limit_bytes=VMEM_LIMIT)


def _dot(a, b):
    return jnp.dot(a, b, preferred_element_type=F32)


def _silu(x):
    return x * jax.nn.sigmoid(x)


def _norm_mod(x, g, scale, shift):
    ms = jnp.mean(x * x, axis=-1, keepdims=True)
    h = x * lax.rsqrt(ms + NORM_EPS) * g
    return h * (1.0 + scale) + shift


def _ada_kernel(c_ref, w_ref, b_ref, o_ref):
    c = c_ref[...]
    o_ref[0] = _dot(_silu(c).astype(BF16), w_ref[0].astype(BF16)) + b_ref[0]


def _ada_mod(c_pad, w_ada, b_ada):
    L, D, N = w_ada.shape
    Bp = c_pad.shape[0]
    tn = D_MODEL
    return pl.pallas_call(
        _ada_kernel,
        grid=(L, N // tn),
        in_specs=[
            pl.BlockSpec((Bp, D), lambda l, j: (0, 0)),
            pl.BlockSpec((1, D, tn), lambda l, j: (l, 0, j)),
            pl.BlockSpec((1, 1, tn), lambda l, j: (l, 0, j)),
        ],
        out_specs=pl.BlockSpec((1, Bp, tn), lambda l, j: (l, 0, j)),
        out_shape=jax.ShapeDtypeStruct((L, Bp, N), F32),
        compiler_params=_cparams(("arbitrary", "arbitrary")),
        name="ada",
    )(c_pad, w_ada, b_ada.reshape(L, 1, N))


def _segment_mean_sq(x, ones_bd, width):
    sq = x * x
    hi = sq.astype(BF16)
    lo = (sq - hi.astype(F32)).astype(BF16)
    return (_dot(hi, ones_bd) + _dot(lo, ones_bd)) * (1.0 / width)


def _rope_halves(x, cos2, sin_signed, half):
    w = x.shape[-1]
    lane = lax.broadcasted_iota(jnp.int32, x.shape, 1)
    first = (lane % (2 * half)) < half
    rot = jnp.where(first, pltpu.roll(x, w - half, 1), pltpu.roll(x, half, 1))
    return x * cos2 + rot * sin_signed


def _inproj_even_kernel(x_ref, g_ref, sc_ref, sh_ref, w_ref, ones_ref, qn_ref, kn_ref,
                        cos_ref, sin_ref, rg_ref, q_ref, k_ref, v_ref):
    h = _norm_mod(x_ref[0], g_ref[...], sc_ref[0], sh_ref[0]).astype(BF16)
    rg_ref[0] = _dot(h, w_ref[:, 0:2 * D_RNN])
    c0 = 2 * D_RNN
    q = _dot(h, w_ref[:, c0:c0 + D_ATT])
    c1 = c0 + D_ATT
    kw = KV_HEADS * HEAD_DIM
    k = _dot(h, w_ref[:, c1:c1 + kw])
    v_ref[0] = _dot(h, w_ref[:, c1 + kw:c1 + 2 * kw]).astype(BF16)

    half = HEAD_DIM // 2
    ones = ones_ref[...]
    qn = q * lax.rsqrt(_segment_mean_sq(q, ones, HEAD_DIM) + NORM_EPS) * qn_ref[...]
    qr = _rope_halves(qn, cos_ref[...], sin_ref[...], half)
    q_ref[0] = (qr * ATT_Q_SCALE).astype(BF16)
    kn = k * lax.rsqrt(_segment_mean_sq(k, ones[:kw, :kw], HEAD_DIM) + NORM_EPS) * kn_ref[...]
    kr = _rope_halves(kn, cos_ref[:, :kw], sin_ref[:, :kw], half)
    k_ref[0] = kr.astype(BF16)


def _inproj_even(x, g, scale, shift, w, ones_bd, qn, kn, cos_t, sin_t):
    B, S, D = x.shape
    tm = ROW_TILE
    N = w.shape[1]
    kw = KV_HEADS * HEAD_DIM
    row = lambda b, i: (b, i, 0)
    per_b = lambda b, i: (b, 0, 0)
    const2 = lambda b, i: (0, 0)
    return pl.pallas_call(
        _inproj_even_kernel,
        grid=(B, S // tm),
        in_specs=[
            pl.BlockSpec((1, tm, D), row),
            pl.BlockSpec((1, D), const2),
            pl.BlockSpec((1, 1, D), per_b),
            pl.BlockSpec((1, 1, D), per_b),
            pl.BlockSpec((D, N), const2),
            pl.BlockSpec((D_ATT, D_ATT), const2),
            pl.BlockSpec((1, D_ATT), const2),
            pl.BlockSpec((1, kw), const2),
            pl.BlockSpec((tm, D_ATT), lambda b, i: (i, 0)),
            pl.BlockSpec((tm, D_ATT), lambda b, i: (i, 0)),
        ],
        out_specs=[
            pl.BlockSpec((1, tm, 2 * D_RNN), row),
            pl.BlockSpec((1, tm, D_ATT), row),
            pl.BlockSpec((1, tm, kw), row),
            pl.BlockSpec((1, tm, kw), row),
        ],
        out_shape=[
            jax.ShapeDtypeStruct((B, S, 2 * D_RNN), F32),
            jax.ShapeDtypeStruct((B, S, D_ATT), BF16),
            jax.ShapeDtypeStruct((B, S, kw), BF16),
            jax.ShapeDtypeStruct((B, S, kw), BF16),
        ],
        compiler_params=_cparams(("arbitrary", "arbitrary")),
        name="inproj_even",
    )(x, g, scale, shift, w, ones_bd, qn, kn, cos_t, sin_t)


def _scan_chunk(a, b, carry, reverse):
    R = a.shape[0]
    row = lax.broadcasted_iota(jnp.int32, a.shape, 0) % SUBLANES
    s = 1
    while s < SUBLANES:
        if reverse:
            keep = row < SUBLANES - s
            shift = R - s
        else:
            keep = row >= s
            shift = s
        a_sh = jnp.where(keep, pltpu.roll(a, shift, 0), 1.0)
        b_sh = jnp.where(keep, pltpu.roll(b, shift, 0), 0.0)
        b = a * b_sh + b
        a = a * a_sh
        s *= 2
    n_groups = R // SUBLANES
    hs = [None] * n_groups
    for gi in (range(n_groups - 1, -1, -1) if reverse else range(n_groups)):
        rows = slice(gi * SUBLANES, (gi + 1) * SUBLANES)
        h = a[rows] * carry + b[rows]
        carry = h[0:1, :] if reverse else h[SUBLANES - 1:SUBLANES, :]
        hs[gi] = h
    return jnp.concatenate(hs, axis=0), carry


def _rglru_kernel(x_ref, gate_ref, cw_ref, cb_ref, gw_ref, gb_ref, lam_ref, y_ref, hf_ref, carry_ref):
    S = x_ref.shape[1]
    R = SCAN_ROWS
    n_chunks = S // R
    cw = cw_ref[...]
    cb = cb_ref[...]

    def conv_chunk(c):
        r0 = pl.multiple_of(c * R, R)
        main = x_ref[0, pl.ds(r0, R), :]
        lo = pl.multiple_of(jnp.maximum(r0 - SUBLANES, 0), SUBLANES)
        hi = pl.multiple_of(jnp.minimum(r0 + R, S - SUBLANES), SUBLANES)
        prev = jnp.where(c > 0, x_ref[0, pl.ds(lo, SUBLANES), :], 0.0)
        nxt = jnp.where(c < n_chunks - 1, x_ref[0, pl.ds(hi, SUBLANES), :], 0.0)
        ext = jnp.concatenate([prev, main, nxt], axis=0)
        xc = cb
        for tap in range(CONV_W):
            o = SUBLANES - 2 + tap
            xc = xc + ext[o:o + R, :] * cw[tap:tap + 1, :]
        return r0, xc

    def direction(c, d, reverse):
        r0, xc = conv_chunk(c)
        xb = xc.astype(BF16)
        r = jax.nn.sigmoid(_dot(xb, gw_ref[d, 0, 0]) + gb_ref[d, 0])
        i = jax.nn.sigmoid(_dot(xb, gw_ref[d, 1, 0]) + gb_ref[d, 1])
        log_a = -RG_C * r * jax.nn.softplus(-lam_ref[d])
        a = jnp.exp(log_a)
        b = jnp.sqrt(jnp.tanh(-log_a) * (a * a + 1.0)) * (i * xc)
        h, carry = _scan_chunk(a, b, carry_ref[...], reverse)
        carry_ref[...] = carry
        return r0, h

    carry_ref[...] = jnp.zeros_like(carry_ref)

    def fwd_body(c, _):
        r0, h = direction(c, 0, False)
        hf_ref[pl.ds(r0, R), :] = h
        return 0

    lax.fori_loop(0, n_chunks, fwd_body, 0, unroll=2)
    carry_ref[...] = jnp.zeros_like(carry_ref)

    def bwd_body(j, _):
        c = n_chunks - 1 - j
        r0, h = direction(c, 1, True)
        gate = gate_ref[0, pl.ds(r0, R), :]
        y = jax.nn.gelu(gate) * (hf_ref[pl.ds(r0, R), :] + h)
        y_ref[0, pl.ds(r0, R), :] = y.astype(y_ref.dtype)
        return 0

    lax.fori_loop(0, n_chunks, bwd_body, 0, unroll=2)


def _rglru(rg, conv_w, conv_b, gw_bd, gate_b, lam):
    B, S, _ = rg.shape
    n_lg = D_RNN // LANES
    return pl.pallas_call(
        _rglru_kernel,
        grid=(B, n_lg),
        in_specs=[
            pl.BlockSpec((1, S, LANES), lambda b, j: (b, 0, n_lg + j)),
            pl.BlockSpec((1, S, LANES), lambda b, j: (b, 0, j)),
            pl.BlockSpec((CONV_W, LANES), lambda b, j: (0, j)),
            pl.BlockSpec((1, LANES), lambda b, j: (0, j)),
            pl.BlockSpec((2, 2, 1, LANES, LANES), lambda b, j: (0, 0, j, 0, 0)),
            pl.BlockSpec((2, 2, 1, LANES), lambda b, j: (0, 0, 0, j)),
            pl.BlockSpec((2, 1, LANES), lambda b, j: (0, 0, j)),
        ],
        out_specs=pl.BlockSpec((1, S, LANES), lambda b, j: (b, 0, j)),
        out_shape=jax.ShapeDtypeStruct((B, S, D_RNN), BF16),
        scratch_shapes=[pltpu.VMEM((S, LANES), F32), pltpu.VMEM((1, LANES), F32)],
        compiler_params=_cparams(("arbitrary", "arbitrary")),
        name="rglru",
    )(rg, rg, conv_w, conv_b, gw_bd, gate_b, lam)


def _attn_kernel(q_ref, kt_ref, v_ref, o_ref, m_ref, acc_ref):
    S = kt_ref.shape[3]
    G = ATT_HEADS // KV_HEADS
    reps = ATT_TK // LANES
    qs = [q_ref[0, :, g * HEAD_DIM:(g + 1) * HEAD_DIM] for g in range(G)]
    m_ref[...] = jnp.full_like(m_ref, NEG_BIG)
    acc_ref[...] = jnp.zeros_like(acc_ref)

    def scores(t):
        j, g = divmod(t, G)
        return _dot(qs[g], kt_ref[0, 0, :, j * ATT_TK:(j + 1) * ATT_TK])

    n_items = (S // ATT_TK) * G
    s_next = scores(0)
    for t in range(n_items):
        j, g = divmod(t, G)
        s = s_next
        if t + 1 < n_items:
            s_next = scores(t + 1)
        m_old = m_ref[g]
        m_new = jnp.maximum(m_old, jnp.max(s, axis=1, keepdims=True))
        p = jnp.exp2(s - jnp.concatenate([m_new] * reps, axis=1))
        pv = _dot(p.astype(BF16), v_ref[0, 0, j * ATT_TK:(j + 1) * ATT_TK, :])
        acc_ref[g] = jnp.exp2(m_old - m_new) * acc_ref[g] + pv
        m_ref[g] = m_new
    outs = []
    for g in range(G):
        acc = acc_ref[g]
        outs.append(acc[:, :HEAD_DIM] / acc[:, HEAD_DIM:2 * HEAD_DIM])
    o_ref[0] = jnp.concatenate(outs, axis=1).astype(o_ref.dtype)


def _attention(q, kt, v_ext):
    B, S, _ = q.shape
    G = ATT_HEADS // KV_HEADS
    tq = ATT_TQ
    qw = G * HEAD_DIM
    return pl.pallas_call(
        _attn_kernel,
        grid=(B, KV_HEADS, S // tq),
        in_specs=[
            pl.BlockSpec((1, tq, qw), lambda b, h, i: (b, i, h)),
            pl.BlockSpec((1, 1, HEAD_DIM, S), lambda b, h, i: (b, h, 0, 0)),
            pl.BlockSpec((1, 1, S, LANES), lambda b, h, i: (b, h, 0, 0)),
        ],
        out_specs=pl.BlockSpec((1, tq, qw), lambda b, h, i: (b, i, h)),
        out_shape=jax.ShapeDtypeStruct((B, S, D_ATT), BF16),
        scratch_shapes=[
            pltpu.VMEM((G, tq, LANES), F32),
            pltpu.VMEM((G, tq, LANES), F32),
        ],
        compiler_params=_cparams(("arbitrary", "arbitrary", "arbitrary")),
        name="attn",
    )(q, kt, v_ext)


def _attn_kv_layout(k, v):
    B, S, _ = k.shape
    kt = k.reshape(B, S, KV_HEADS, HEAD_DIM).transpose(0, 2, 3, 1)
    vh = v.reshape(B, S, KV_HEADS, HEAD_DIM).transpose(0, 2, 1, 3)
    ones = jnp.ones((B, KV_HEADS, S, LANES - HEAD_DIM), v.dtype)
    return kt, jnp.concatenate([vh, ones], axis=-1)


def _first_argmax(x, iota, size):
    m = jnp.max(x, axis=0, keepdims=True)
    return m, jnp.min(jnp.where(x == m, iota, size), axis=0, keepdims=True)


def _route_choices(hb, rwt, rb):
    tm = hb.shape[0]
    logits = lax.dot_general(rwt, hb, (((1,), (1,)), ((), ())), preferred_element_type=F32)
    scores = jax.nn.sigmoid(logits)
    biased = scores + rb
    iota_e = lax.broadcasted_iota(jnp.int32, (EPG, tm), 0).astype(F32)
    group_scores = []
    for gi in range(N_GROUPS):
        bg = biased[gi * EPG:(gi + 1) * EPG, :]
        m1, i1 = _first_argmax(bg, iota_e, EPG)
        m2 = jnp.max(jnp.where(iota_e == i1, -jnp.inf, bg), axis=0, keepdims=True)
        group_scores.append(m1 + m2)
    gs = jnp.concatenate(group_scores, axis=0)
    iota_g = lax.broadcasted_iota(jnp.int32, (N_GROUPS, tm), 0).astype(F32)
    _, g_sel = _first_argmax(gs, iota_g, N_GROUPS)
    in_b = jnp.zeros((EPG, tm), F32)
    in_s = jnp.zeros((EPG, tm), F32)
    for gi in range(N_GROUPS):
        sel = g_sel == gi
        in_b = jnp.where(sel, biased[gi * EPG:(gi + 1) * EPG, :], in_b)
        in_s = jnp.where(sel, scores[gi * EPG:(gi + 1) * EPG, :], in_s)
    _, l1 = _first_argmax(in_b, iota_e, EPG)
    _, l2 = _first_argmax(jnp.where(iota_e == l1, -jnp.inf, in_b), iota_e, EPG)
    w1 = jnp.sum(jnp.where(iota_e == l1, in_s, 0.0), axis=0, keepdims=True)
    w2 = jnp.sum(jnp.where(iota_e == l2, in_s, 0.0), axis=0, keepdims=True)
    wsum = w1 + w2
    experts = jnp.concatenate([g_sel * EPG + l1, g_sel * EPG + l2], axis=0)
    return experts, jnp.concatenate([w1 / wsum, w2 / wsum], axis=0)


def _outproj_route_kernel(*refs, n_in):
    y_refs = refs[:n_in]
    (w_ref, x_ref, gate_ref, g2_ref, sc2_ref, sh2_ref, rwt_ref, rb_ref, tri_ref,
     o_ref, idx_ref, wt_ref, rank_ref, cnt_ref, run_ref) = refs[n_in:]
    acc = None
    off = 0
    for yr in y_refs:
        kd = yr.shape[-1]
        part = _dot(yr[0], w_ref[off:off + kd, :])
        acc = part if acc is None else acc + part
        off += kd
    x_new = x_ref[0] + gate_ref[0] * acc
    o_ref[0] = x_new

    hb = _norm_mod(x_new, g2_ref[...], sc2_ref[0], sh2_ref[0]).astype(BF16)
    tm = hb.shape[0]
    experts, weights = _route_choices(hb, rwt_ref[...], rb_ref[...])
    idx_ref[...] = experts.astype(jnp.int32)
    wt_ref[...] = weights

    @pl.when((pl.program_id(0) == 0) & (pl.program_id(1) == 0))
    def _():
        run_ref[...] = jnp.zeros_like(run_ref)

    iota = lax.broadcasted_iota(jnp.int32, (N_EXPERTS, tm), 0).astype(F32)
    oh0 = jnp.where(iota == experts[0:1, :], 1.0, 0.0)
    oh1 = jnp.where(iota == experts[1:2, :], 1.0, 0.0)
    tri = tri_ref[...]
    pre0 = _dot(oh0.astype(BF16), tri)
    pre1 = _dot(oh1.astype(BF16), tri)
    tot0 = jnp.sum(oh0, axis=1, keepdims=True)
    tot1 = jnp.sum(oh1, axis=1, keepdims=True)
    run = run_ref[...]
    rank0 = jnp.sum(oh0 * (run + pre0), axis=0, keepdims=True)
    rank1 = jnp.sum(oh1 * (run + tot0 + pre1), axis=0, keepdims=True)
    rank_ref[...] = jnp.concatenate([rank0, rank1], axis=0).astype(jnp.int32)
    run = run + tot0 + tot1
    run_ref[...] = run
    cnt_ref[...] = run


def _outproj_route(ys, w, x, gate, g2, scale2, shift2, rwt, rb):
    B, S, D = x.shape
    tm = ROW_TILE
    n = S // tm
    row = lambda b, i: (b, i, 0)
    per_b = lambda b, i: (b, 0, 0)
    const2 = lambda b, i: (0, 0)
    tok = lambda b, i: (0, b * n + i)
    tri = (jnp.arange(tm)[:, None] < jnp.arange(tm)[None, :]).astype(BF16)
    in_specs = [pl.BlockSpec((1, tm, y.shape[-1]), row) for y in ys]
    in_specs += [
        pl.BlockSpec(w.shape, const2),
        pl.BlockSpec((1, tm, D), row),
        pl.BlockSpec((1, 1, D), per_b),
        pl.BlockSpec((1, D), const2),
        pl.BlockSpec((1, 1, D), per_b),
        pl.BlockSpec((1, 1, D), per_b),
        pl.BlockSpec((N_EXPERTS, D), const2),
        pl.BlockSpec((N_EXPERTS, 1), const2),
        pl.BlockSpec((tm, tm), const2),
    ]
    return pl.pallas_call(
        functools.partial(_outproj_route_kernel, n_in=len(ys)),
        grid=(B, n),
        in_specs=in_specs,
        out_specs=[
            pl.BlockSpec((1, tm, D), row),
            pl.BlockSpec((TOP_K, tm), tok),
            pl.BlockSpec((TOP_K, tm), tok),
            pl.BlockSpec((TOP_K, tm), tok),
            pl.BlockSpec((N_EXPERTS, 1), const2),
        ],
        out_shape=[
            jax.ShapeDtypeStruct((B, S, D), F32),
            jax.ShapeDtypeStruct((TOP_K, B * S), jnp.int32),
            jax.ShapeDtypeStruct((TOP_K, B * S), F32),
            jax.ShapeDtypeStruct((TOP_K, B * S), jnp.int32),
            jax.ShapeDtypeStruct((N_EXPERTS, 1), F32),
        ],
        scratch_shapes=[pltpu.VMEM((N_EXPERTS, 1), F32)],
        compiler_params=_cparams(("arbitrary", "arbitrary")),
        name="outproj_route",
    )(*ys, w, x, gate, g2, scale2, shift2, rwt, rb, tri)


def _inproj_ret_kernel(x_ref, g_ref, sc_ref, sh_ref, w_ref, cos_ref, sin_ref,
                       q_ref, k_ref, v_ref, gate_ref):
    h = _norm_mod(x_ref[0], g_ref[...], sc_ref[0], sh_ref[0]).astype(BF16)
    cos = cos_ref[...]
    sin = sin_ref[...]
    half = RET_QK // 2
    dqk = RET_HEADS * RET_QK
    dv = RET_HEADS * RET_V

    def rope(t):
        t1 = t[:, :half]
        t2 = t[:, half:]
        return jnp.concatenate([t1 * cos - t2 * sin, t2 * cos + t1 * sin], axis=-1)

    for hd in range(RET_HEADS):
        c = hd * RET_QK
        q_ref[0, :, c:c + RET_QK] = rope(_dot(h, w_ref[:, c:c + RET_QK])).astype(BF16)
        kh = rope(_dot(h, w_ref[:, dqk + c:dqk + c + RET_QK])) * (RET_QK ** -0.5)
        k_ref[0, :, c:c + RET_QK] = kh.astype(BF16)
    for hd in range(RET_HEADS):
        c = hd * RET_V
        v_ref[0, :, c:c + RET_V] = _dot(h, w_ref[:, 2 * dqk + c:2 * dqk + c + RET_V]).astype(BF16)
        gate_ref[0, :, c:c + RET_V] = _dot(h, w_ref[:, 2 * dqk + dv + c:2 * dqk + dv + c + RET_V])


def _inproj_ret(x, g, scale, shift, w, cos_t, sin_t):
    B, S, D = x.shape
    tm = ROW_TILE
    N = w.shape[1]
    dqk = RET_HEADS * RET_QK
    dv = RET_HEADS * RET_V
    half = RET_QK // 2
    row = lambda b, i: (b, i, 0)
    per_b = lambda b, i: (b, 0, 0)
    const2 = lambda b, i: (0, 0)
    return pl.pallas_call(
        _inproj_ret_kernel,
        grid=(B, S // tm),
        in_specs=[
            pl.BlockSpec((1, tm, D), row),
            pl.BlockSpec((1, D), const2),
            pl.BlockSpec((1, 1, D), per_b),
            pl.BlockSpec((1, 1, D), per_b),
            pl.BlockSpec((D, N), const2),
            pl.BlockSpec((tm, half), lambda b, i: (i, 0)),
            pl.BlockSpec((tm, half), lambda b, i: (i, 0)),
        ],
        out_specs=[
            pl.BlockSpec((1, tm, dqk), row),
            pl.BlockSpec((1, tm, dqk), row),
            pl.BlockSpec((1, tm, dv), row),
            pl.BlockSpec((1, tm, dv), row),
        ],
        out_shape=[
            jax.ShapeDtypeStruct((B, S, dqk), BF16),
            jax.ShapeDtypeStruct((B, S, dqk), BF16),
            jax.ShapeDtypeStruct((B, S, dv), BF16),
            jax.ShapeDtypeStruct((B, S, dv), F32),
        ],
        compiler_params=_cparams(("arbitrary", "arbitrary")),
        name="inproj_ret",
    )(x, g, scale, shift, w, cos_t, sin_t)


def _ret_kernel(*refs, reverse):
    if reverse:
        q_ref, kt_ref, v_ref, dm_ref, qd_ref, kd_ref, cd_ref, yf_ref, g_ref, o_ref, st_ref = refs
    else:
        q_ref, kt_ref, v_ref, dm_ref, qd_ref, kd_ref, cd_ref, o_ref, st_ref = refs
    C = RET_CHUNK
    n_sub = q_ref.shape[1] // C

    @pl.when(pl.program_id(2) == 0)
    def _():
        st_ref[...] = jnp.zeros_like(st_ref)

    dm = dm_ref[0]
    qd = qd_ref[0]
    kd = kd_ref[0]
    cd = cd_ref[0]
    order = range(n_sub - 1, -1, -1) if reverse else range(n_sub)
    for ci in order:
        r = slice(ci * C, (ci + 1) * C)
        qc = q_ref[0, r, :]
        ktc = kt_ref[0, 0, :, r]
        vc = v_ref[0, r, :]
        st = st_ref[...]
        scores = _dot(qc, ktc) * dm
        out = _dot(scores.astype(BF16), vc) + _dot(qc, st.astype(BF16)) * qd
        st_ref[...] = st * cd + _dot((ktc.astype(F32) * kd).astype(BF16), vc)
        if reverse:
            y = yf_ref[0, r, :] + out
            y = y * lax.rsqrt(jnp.mean(y * y, axis=-1, keepdims=True) + NORM_EPS)
            o_ref[0, r, :] = (y * _silu(g_ref[0, r, :])).astype(o_ref.dtype)
        else:
            o_ref[0, r, :] = out


def _retention_pass(q, kt, v, tabs, reverse, yf=None, gate=None):
    B, S, _ = q.shape
    ts = min(RET_STEP_ROWS, S)
    n = S // ts
    C = RET_CHUNK
    step = (lambda i: n - 1 - i) if reverse else (lambda i: i)
    rows = lambda width: pl.BlockSpec((1, ts, width), lambda b, h, i: (b, step(i), h))
    per_h = lambda shape: pl.BlockSpec((1,) + shape, lambda b, h, i: (h, 0, 0))
    in_specs = [
        rows(RET_QK),
        pl.BlockSpec((1, 1, RET_QK, ts), lambda b, h, i: (b, h, 0, step(i))),
        rows(RET_V),
        per_h((C, C)), per_h((C, 1)), per_h((1, C)), per_h((1, 1)),
    ]
    args = [q, kt, v, *tabs]
    if reverse:
        in_specs += [rows(RET_V), rows(RET_V)]
        args += [yf, gate]
    return pl.pallas_call(
        functools.partial(_ret_kernel, reverse=reverse),
        grid=(B, RET_HEADS, n),
        in_specs=in_specs,
        out_specs=rows(RET_V),
        out_shape=jax.ShapeDtypeStruct((B, S, RET_HEADS * RET_V), BF16 if reverse else F32),
        scratch_shapes=[pltpu.VMEM((RET_QK, RET_V), F32)],
        compiler_params=_cparams(("arbitrary", "arbitrary", "arbitrary")),
        name="retention_bwd" if reverse else "retention_fwd",
    )(*args)


def _retention_tables(offset, strict):
    C = RET_CHUNK
    log_gamma = jnp.log1p(-jnp.exp2(-5.0 - offset - jnp.arange(RET_HEADS, dtype=F32)))
    idx = jnp.arange(C, dtype=F32)
    diff = idx[:, None] - idx[None, :]
    mask = (diff > 0) if strict else (diff >= 0)
    d_intra = jnp.where(mask, jnp.exp(jnp.where(mask, diff, 0.0)[None] * log_gamma[:, None, None]), 0.0)
    q_dec = jnp.exp((idx + 1.0)[None, :] * log_gamma[:, None])
    k_dec = jnp.exp((C - 1.0 - idx)[None, :] * log_gamma[:, None])
    chunk_dec = jnp.exp(C * log_gamma)
    return d_intra, q_dec, k_dec, chunk_dec


def _shape_tables(d_intra, q_dec, k_dec, chunk_dec, flip):
    if flip:
        d_intra = d_intra[:, ::-1, ::-1]
        q_dec = q_dec[:, ::-1]
        k_dec = k_dec[:, ::-1]
    return (d_intra, q_dec[:, :, None], k_dec[:, None, :], chunk_dec[:, None, None])


def _dispatch_plan(idx, rank, counts):
    T = idx.shape[1]
    A = T * TOP_K
    bm = MOE_BM
    nb = (A + N_EXPERTS * bm) // bm
    counts = counts.reshape(N_EXPERTS).astype(jnp.int32)
    padded = ((counts + bm - 1) // bm) * bm
    pad_end = jnp.cumsum(padded)
    pad_start = pad_end - padded
    experts = jnp.arange(N_EXPERTS, dtype=jnp.int32)
    dest = jnp.sum(jnp.where(idx[..., None] == experts, pad_start, 0), axis=-1) + rank
    block_expert = jnp.minimum(
        jnp.searchsorted(pad_end, jnp.arange(nb, dtype=jnp.int32) * bm, side='right'), N_EXPERTS - 1
    ).astype(jnp.int32)
    n_used = (pad_end[-1] // bm).astype(jnp.int32).reshape(1)
    tail_block = jnp.maximum(pad_end // bm - 1, 0).astype(jnp.int32)
    tm = COMB_TM
    dest_tiles = dest.reshape(TOP_K, T // tm, tm).transpose(1, 0, 2).reshape(T // tm, 1, TOP_K * tm)
    return block_expert, n_used, tail_block, dest_tiles


def _zero_block_kernel(tail_ref, o_ref):
    del tail_ref
    o_ref[...] = jnp.zeros_like(o_ref)


def _zero_tail_blocks(tail_block, n_rows, width):
    grid_spec = pltpu.PrefetchScalarGridSpec(
        num_scalar_prefetch=1,
        grid=(N_EXPERTS,),
        in_specs=[],
        out_specs=pl.BlockSpec((MOE_BM, width), lambda e, tail: (tail[e], 0)),
    )
    return pl.pallas_call(
        _zero_block_kernel,
        grid_spec=grid_spec,
        out_shape=jax.ShapeDtypeStruct((n_rows, width), F32),
        compiler_params=_cparams(("arbitrary",)),
        name="zero_tail",
    )(tail_block)


def _row_copy(src, src_row, dst, dst_row, sem):
    return pltpu.make_async_copy(src.at[pl.ds(src_row, 1), :], dst.at[pl.ds(dst_row, 1), :], sem)


def _dispatch_kernel(dest_ref, x_ref, g_ref, sc_ref, sh_ref, gate_ref, sg_ref, su_ref, sd_ref, xd_in_ref,
                     xb_ref, xd_ref, hbuf, sem, *, n_steps):
    del xd_in_ref
    step = pl.program_id(0) * pl.num_programs(1) + pl.program_id(1)
    slot = step % 2
    x = x_ref[0]
    tm = x.shape[0]

    def drain(s):
        for _ in range(TOP_K * tm):
            _row_copy(hbuf.at[s], 0, xd_ref, 0, sem.at[s]).wait()

    @pl.when(step >= 2)
    def _():
        drain(slot)

    h = _norm_mod(x, g_ref[...], sc_ref[0], sh_ref[0])
    hbuf[slot] = h
    for r in range(tm):
        for k in range(TOP_K):
            _row_copy(hbuf.at[slot], r, xd_ref, dest_ref[0, 0, k * tm + r], sem.at[slot]).start()
    hb = h.astype(BF16)
    act = _silu(_dot(hb, sg_ref[...])) * _dot(hb, su_ref[...])
    xb_ref[0] = x + gate_ref[0] * _dot(act.astype(BF16), sd_ref[...])

    @pl.when(step == n_steps - 1)
    def _():
        drain(slot)
        if n_steps > 1:
            drain(1 - slot)


def _dispatch_shared(dest_tiles, x, g, scale, shift, gate, sg, su, sd, x_disp):
    B, S, D = x.shape
    tm = COMB_TM
    n = S // tm
    row = lambda b, i: (b, i, 0)
    per_b = lambda b, i: (b, 0, 0)
    const2 = lambda b, i: (0, 0)
    return pl.pallas_call(
        functools.partial(_dispatch_kernel, n_steps=B * n),
        grid=(B, n),
        in_specs=[
            pl.BlockSpec((1, 1, TOP_K * tm), lambda b, i: (b * n + i, 0, 0), memory_space=pltpu.SMEM),
            pl.BlockSpec((1, tm, D), row),
            pl.BlockSpec((1, D), const2),
            pl.BlockSpec((1, 1, D), per_b),
            pl.BlockSpec((1, 1, D), per_b),
            pl.BlockSpec((1, 1, D), per_b),
            pl.BlockSpec((D, D_FF), const2),
            pl.BlockSpec((D, D_FF), const2),
            pl.BlockSpec((D_FF, D), const2),
            pl.BlockSpec(memory_space=pl.ANY),
        ],
        out_specs=[pl.BlockSpec((1, tm, D), row), pl.BlockSpec(memory_space=pl.ANY)],
        out_shape=[jax.ShapeDtypeStruct((B, S, D), F32), jax.ShapeDtypeStruct(x_disp.shape, F32)],
        input_output_aliases={9: 1},
        scratch_shapes=[pltpu.VMEM((2, tm, D), F32), pltpu.SemaphoreType.DMA((2,))],
        compiler_params=_cparams(("arbitrary", "arbitrary")),
        name="moe_dispatch",
    )(dest_tiles, x, g, scale, shift, gate, sg, su, sd, x_disp)


def _expert_kernel(be_ref, nu_ref, x_ref, wg_ref, wu_ref, wd_ref, o_ref):
    del be_ref

    @pl.when(pl.program_id(0) < nu_ref[0])
    def _():
        xb = x_ref[...].astype(BF16)
        act = _silu(_dot(xb, wg_ref[0])) * _dot(xb, wu_ref[0])
        o_ref[...] = _dot(act.astype(BF16), wd_ref[0])

    @pl.when(pl.program_id(0) >= nu_ref[0])
    def _():
        o_ref[...] = jnp.zeros_like(o_ref)


def _experts(block_expert, n_used, x_disp, wg, wu, wd):
    P, D = x_disp.shape
    bm = MOE_BM
    nb = P // bm
    blk = lambda i, be, nu: (jnp.minimum(i, jnp.maximum(nu[0] - 1, 0)), 0)
    wsel = lambda i, be, nu: (be[jnp.minimum(i, jnp.maximum(nu[0] - 1, 0))], 0, 0)
    grid_spec = pltpu.PrefetchScalarGridSpec(
        num_scalar_prefetch=2,
        grid=(nb,),
        in_specs=[
            pl.BlockSpec((bm, D), blk),
            pl.BlockSpec((1, D, D_FF), wsel),
            pl.BlockSpec((1, D, D_FF), wsel),
            pl.BlockSpec((1, D_FF, D), wsel),
        ],
        out_specs=pl.BlockSpec((bm, D), lambda i, be, nu: (i, 0)),
    )
    return pl.pallas_call(
        _expert_kernel,
        grid_spec=grid_spec,
        out_shape=jax.ShapeDtypeStruct((P, D), F32),
        compiler_params=_cparams(("arbitrary",)),
        name="experts",
    )(block_expert, n_used, x_disp, wg, wu, wd)


def _combine_kernel(pos_cur_ref, pos_next_ref, y_hbm, w_ref, xb_ref, gate_ref, o_ref, ybuf, sem):
    i = pl.program_id(0)
    n = pl.num_programs(0)
    slot = i % 2
    rows = ybuf.shape[1]
    tm = rows // TOP_K

    def issue(pos_ref, dst_slot):
        for r in range(rows):
            _row_copy(y_hbm, pos_ref[0, 0, r], ybuf.at[dst_slot], r, sem.at[dst_slot]).start()

    @pl.when(i == 0)
    def _():
        issue(pos_cur_ref, 0)

    @pl.when(i + 1 < n)
    def _():
        issue(pos_next_ref, 1 - slot)

    for _ in range(rows):
        _row_copy(y_hbm, 0, ybuf.at[slot], 0, sem.at[slot]).wait()
    w = w_ref[...]
    y = ybuf[slot, 0:tm, :] * w[:, 0:1] + ybuf[slot, tm:rows, :] * w[:, 1:2]
    o_ref[...] = xb_ref[...] + gate_ref[0] * y


def _combine(pos_tiles, y_disp, w_cols, xb, gate, tiles_per_seq):
    T, D = xb.shape
    tm = COMB_TM
    n = T // tm
    smem_blk = lambda f: pl.BlockSpec((1, 1, TOP_K * tm), f, memory_space=pltpu.SMEM)
    return pl.pallas_call(
        _combine_kernel,
        grid=(n,),
        in_specs=[
            smem_blk(lambda i: (i, 0, 0)),
            smem_blk(lambda i: (jnp.minimum(i + 1, n - 1), 0, 0)),
            pl.BlockSpec(memory_space=pl.ANY),
            pl.BlockSpec((tm, TOP_K), lambda i: (i, 0)),
            pl.BlockSpec((tm, D), lambda i: (i, 0)),
            pl.BlockSpec((1, 1, D), lambda i: (i // tiles_per_seq, 0, 0)),
        ],
        out_specs=pl.BlockSpec((tm, D), lambda i: (i, 0)),
        out_shape=jax.ShapeDtypeStruct((T, D), F32),
        scratch_shapes=[pltpu.VMEM((2, TOP_K * tm, D), F32), pltpu.SemaphoreType.DMA((2,))],
        compiler_params=_cparams(("arbitrary",)),
        name="combine",
    )(pos_tiles, pos_tiles, y_disp, w_cols, xb, gate)


def _axial_angles(S, hd):
    rows = S // GRID_W
    row = jnp.repeat(jnp.arange(rows), GRID_W).astype(F32)
    col = jnp.tile(jnp.arange(GRID_W), rows).astype(F32)
    n = hd // 4
    inv = ROPE_THETA ** (-jnp.arange(n, dtype=F32) / n)
    ang = jnp.concatenate([row[:, None] * inv, col[:, None] * inv], axis=-1)
    return jnp.cos(ang), jnp.sin(ang)


def _block_diag_gates(gate_w):
    per = LANES // RG_BW
    n_lg = D_RNN // LANES
    g = gate_w.reshape(2, 2, n_lg, per, RG_BW, RG_BW)
    eye = jnp.eye(per, dtype=gate_w.dtype)
    bd = jnp.einsum('dglpij,pq->dglpiqj', g, eye)
    return bd.reshape(2, 2, n_lg, LANES, LANES).astype(BF16)


def kernel(x_prompt, x_sample, c_prompt, c_sample, norm_mix, norm_ffn, w_ada, b_ada, w_in_ab, conv_w,
           conv_b, rg_gate_w, rg_gate_b, rg_lambda, q_norm, k_norm, w_out_ab, w_in_ret, w_out_ret,
           router_w, router_bias, exp_w_gate, exp_w_up, exp_w_down, sh_w_gate, sh_w_up, sh_w_down):
    assert x_prompt.shape[1:] == x_sample.shape[1:]
    Bp_, S, D = x_prompt.shape
    B = Bp_ + x_sample.shape[0]
    T = B * S
    x = jnp.concatenate([x_prompt, x_sample], axis=0)
    c = jnp.concatenate([c_prompt, c_sample], axis=0)
    c_rows = -(-B // 16) * 16
    c_pad = jnp.zeros((c_rows, D), F32).at[:B].set(c)
    mod = _ada_mod(c_pad, w_ada, b_ada)[:, :B].reshape(DEPTH, B, 6, 1, D)

    cos_a, sin_a = _axial_angles(S, HEAD_DIM)
    cos_att = jnp.tile(jnp.concatenate([cos_a, cos_a], axis=-1), (1, ATT_HEADS))
    sin_att = jnp.tile(jnp.concatenate([-sin_a, sin_a], axis=-1), (1, ATT_HEADS))
    cos_r, sin_r = _axial_angles(S, RET_QK)
    seg = jnp.arange(D_ATT) // HEAD_DIM
    ones_bd = (seg[:, None] == seg[None, :]).astype(BF16)
    tabs_f = _shape_tables(*_retention_tables(0.0, False), flip=False)
    tabs_b = _shape_tables(*_retention_tables(0.5, True), flip=True)
    rwt = router_w.T.astype(BF16)
    rb = router_bias.reshape(N_EXPERTS, 1).astype(F32)

    for l in range(DEPTH):
        shift1, scale1, gate1, shift2, scale2, gate2 = (mod[l, :, j] for j in range(6))
        g_mix = norm_mix[l].reshape(1, D)
        if l % 2 == 0:
            e = l // 2
            rg, q, k, v = _inproj_even(
                x, g_mix, scale1, shift1, w_in_ab[e].astype(BF16), ones_bd,
                jnp.tile(q_norm[e], ATT_HEADS).reshape(1, D_ATT),
                jnp.tile(k_norm[e], KV_HEADS).reshape(1, KV_HEADS * HEAD_DIM), cos_att, sin_att)
            y_rg = _rglru(rg, conv_w[e], conv_b[e].reshape(1, D_RNN), _block_diag_gates(rg_gate_w[e]),
                          rg_gate_b[e].reshape(2, 2, 1, D_RNN), rg_lambda[e].reshape(2, 1, D_RNN))
            y_att = _attention(q, *_attn_kv_layout(k, v))
            ys, w_out = [y_rg, y_att], w_out_ab[e]
        else:
            o = l // 2
            q, k, v, gate = _inproj_ret(x, g_mix, scale1, shift1, w_in_ret[o].astype(BF16), cos_r, sin_r)
            kt = k.reshape(B, S, RET_HEADS, RET_QK).transpose(0, 2, 3, 1)
            y_f = _retention_pass(q, kt, v, tabs_f, reverse=False)
            y_n = _retention_pass(q, kt, v, tabs_b, reverse=True, yf=y_f, gate=gate)
            ys, w_out = [y_n], w_out_ret[o]

        g_ffn = norm_ffn[l].reshape(1, D)
        x, idx, wts, rank, counts = _outproj_route(
            ys, w_out.astype(BF16), x, gate1, g_ffn, scale2, shift2, rwt, rb)
        block_expert, n_used, tail_block, dest_tiles = _dispatch_plan(idx, rank, counts)
        x_disp = _zero_tail_blocks(tail_block, T * TOP_K + N_EXPERTS * MOE_BM, D)
        xb, x_disp = _dispatch_shared(
            dest_tiles, x, g_ffn, scale2, shift2, gate2,
            sh_w_gate[l].astype(BF16), sh_w_up[l].astype(BF16), sh_w_down[l].astype(BF16), x_disp)
        y_disp = _experts(block_expert, n_used, x_disp,
                          exp_w_gate[l].astype(BF16), exp_w_up[l].astype(BF16), exp_w_down[l].astype(BF16))
        x = _combine(dest_tiles, y_disp, wts.T, xb.reshape(T, D), gate2, S // COMB_TM).reshape(B, S, D)

    return x[:Bp_], x[Bp_:]
```

```python
import functools

import jax
import jax.numpy as jnp
from jax import lax
from jax.experimental import pallas as pl
from jax.experimental.pallas import tpu as pltpu

F32 = jnp.float32
BF16 = jnp.bfloat16

D_MODEL = 1024
DEPTH = 4
GRID_W = 64
NORM_EPS = 1e-6
ROPE_THETA = 10000.0
D_RNN = 512
RG_BW = 64
CONV_W = 4
RG_C = 8.0
ATT_HEADS = 8
KV_HEADS = 2
HEAD_DIM = 64
D_ATT = ATT_HEADS * HEAD_DIM
RET_HEADS = 4
RET_QK = 256
RET_V = 512
RET_CHUNK = 128
N_EXPERTS = 32
N_GROUPS = 4
EPG = N_EXPERTS // N_GROUPS
TOP_K = 2
D_FF = 512

LANES = 128
SUBLANES = 8
VMEM_LIMIT = 52 * 1024 * 1024

ROW_TILE = 256
ATT_TQ = 256
ATT_TK = 256
ATT_LOOKAHEAD = 6
ATT_ONES_ROWS = 16
SCAN_ROWS = 128
RET_STEP_ROWS = 1024
MOE_BM = 512
COMB_TM = 256
NEG_BIG = -1e30
ATT_Q_SCALE = HEAD_DIM ** -0.5 * 1.4426950408889634


def _cparams(sem):
    return pltpu.CompilerParams(dimension_semantics=sem, vmem_limit_bytes=VMEM_LIMIT)


def _dot(a, b):
    return jnp.dot(a, b, preferred_element_type=F32)


def _silu(x):
    return x * jax.nn.sigmoid(x)


def _norm_mod(x, g, scale, shift):
    ms = jnp.mean(x * x, axis=-1, keepdims=True)
    h = x * lax.rsqrt(ms + NORM_EPS) * g
    return h * (1.0 + scale) + shift


def _ada_kernel(c_ref, w_ref, b_ref, o_ref):
    c = c_ref[...]
    o_ref[0] = _dot(_silu(c).astype(BF16), w_ref[0].astype(BF16)) + b_ref[0]


def _ada_mod(c_pad, w_ada, b_ada):
    L, D, N = w_ada.shape
    Bp = c_pad.shape[0]
    tn = D_MODEL
    return pl.pallas_call(
        _ada_kernel,
        grid=(L, N // tn),
        in_specs=[
            pl.BlockSpec((Bp, D), lambda l, j: (0, 0)),
            pl.BlockSpec((1, D, tn), lambda l, j: (l, 0, j)),
            pl.BlockSpec((1, 1, tn), lambda l, j: (l, 0, j)),
        ],
        out_specs=pl.BlockSpec((1, Bp, tn), lambda l, j: (l, 0, j)),
        out_shape=jax.ShapeDtypeStruct((L, Bp, N), F32),
        compiler_params=_cparams(("arbitrary", "arbitrary")),
        name="ada",
    )(c_pad, w_ada, b_ada.reshape(L, 1, N))


def _segment_mean_sq(x, ones_bd, width):
    sq = x * x
    hi = sq.astype(BF16)
    lo = (sq - hi.astype(F32)).astype(BF16)
    return (_dot(hi, ones_bd) + _dot(lo, ones_bd)) * (1.0 / width)


def _rope_halves(x, cos2, sin_signed, half):
    w = x.shape[-1]
    lane = lax.broadcasted_iota(jnp.int32, x.shape, 1)
    first = (lane % (2 * half)) < half
    rot = jnp.where(first, pltpu.roll(x, w - half, 1), pltpu.roll(x, half, 1))
    return x * cos2 + rot * sin_signed


def _inproj_even_kernel(x_ref, g_ref, sc_ref, sh_ref, w_ref, ones_ref, qn_ref, kn_ref,
                        cos_ref, sin_ref, rg_ref, q_ref, k_ref, v_ref):
    h = _norm_mod(x_ref[0], g_ref[...], sc_ref[0], sh_ref[0]).astype(BF16)
    rg_ref[0] = _dot(h, w_ref[:, 0:2 * D_RNN])
    c0 = 2 * D_RNN
    q = _dot(h, w_ref[:, c0:c0 + D_ATT])
    c1 = c0 + D_ATT
    kw = KV_HEADS * HEAD_DIM
    k = _dot(h, w_ref[:, c1:c1 + kw])
    v_ref[0] = _dot(h, w_ref[:, c1 + kw:c1 + 2 * kw]).astype(BF16)

    half = HEAD_DIM // 2
    ones = ones_ref[...]
    qn = q * lax.rsqrt(_segment_mean_sq(q, ones, HEAD_DIM) + NORM_EPS) * qn_ref[...]
    qr = _rope_halves(qn, cos_ref[...], sin_ref[...], half)
    q_ref[0] = (qr * ATT_Q_SCALE).astype(BF16)
    kn = k * lax.rsqrt(_segment_mean_sq(k, ones[:kw, :kw], HEAD_DIM) + NORM_EPS) * kn_ref[...]
    kr = _rope_halves(kn, cos_ref[:, :kw], sin_ref[:, :kw], half)
    k_ref[0] = kr.astype(BF16)


def _inproj_even(x, g, scale, shift, w, ones_bd, qn, kn, cos_t, sin_t):
    B, S, D = x.shape
    tm = ROW_TILE
    N = w.shape[1]
    kw = KV_HEADS * HEAD_DIM
    row = lambda b, i: (b, i, 0)
    per_b = lambda b, i: (b, 0, 0)
    const2 = lambda b, i: (0, 0)
    return pl.pallas_call(
        _inproj_even_kernel,
        grid=(B, S // tm),
        in_specs=[
            pl.BlockSpec((1, tm, D), row),
            pl.BlockSpec((1, D), const2),
            pl.BlockSpec((1, 1, D), per_b),
            pl.BlockSpec((1, 1, D), per_b),
            pl.BlockSpec((D, N), const2),
            pl.BlockSpec((D_ATT, D_ATT), const2),
            pl.BlockSpec((1, D_ATT), const2),
            pl.BlockSpec((1, kw), const2),
            pl.BlockSpec((tm, D_ATT), lambda b, i: (i, 0)),
            pl.BlockSpec((tm, D_ATT), lambda b, i: (i, 0)),
        ],
        out_specs=[
            pl.BlockSpec((1, tm, 2 * D_RNN), row),
            pl.BlockSpec((1, tm, D_ATT), row),
            pl.BlockSpec((1, tm, kw), row),
            pl.BlockSpec((1, tm, kw), row),
        ],
        out_shape=[
            jax.ShapeDtypeStruct((B, S, 2 * D_RNN), F32),
            jax.ShapeDtypeStruct((B, S, D_ATT), BF16),
            jax.ShapeDtypeStruct((B, S, kw), BF16),
            jax.ShapeDtypeStruct((B, S, kw), BF16),
        ],
        compiler_params=_cparams(("arbitrary", "arbitrary")),
        name="inproj_even",
    )(x, g, scale, shift, w, ones_bd, qn, kn, cos_t, sin_t)


def _scan_chunk(a, b, carry, reverse):
    R = a.shape[0]
    row = lax.broadcasted_iota(jnp.int32, a.shape, 0) % SUBLANES
    s = 1
    while s < SUBLANES:
        if reverse:
            keep = row < SUBLANES - s
            shift = R - s
        else:
            keep = row >= s
            shift = s
        a_sh = jnp.where(keep, pltpu.roll(a, shift, 0), 1.0)
        b_sh = jnp.where(keep, pltpu.roll(b, shift, 0), 0.0)
        b = a * b_sh + b
        a = a * a_sh
        s *= 2
    n_groups = R // SUBLANES
    hs = [None] * n_groups
    for gi in (range(n_groups - 1, -1, -1) if reverse else range(n_groups)):
        rows = slice(gi * SUBLANES, (gi + 1) * SUBLANES)
        h = a[rows] * carry + b[rows]
        carry = h[0:1, :] if reverse else h[SUBLANES - 1:SUBLANES, :]
        hs[gi] = h
    return jnp.concatenate(hs, axis=0), carry


def _rglru_kernel(x_ref, gate_ref, cw_ref, cb_ref, gw_ref, gb_ref, lam_ref, y_ref, hf_ref, carry_ref):
    S = x_ref.shape[1]
    R = SCAN_ROWS
    n_chunks = S // R
    cw = cw_ref[...]
    cb = cb_ref[...]

    def conv_chunk(c):
        r0 = pl.multiple_of(c * R, R)
        main = x_ref[0, pl.ds(r0, R), :]
        lo = pl.multiple_of(jnp.maximum(r0 - SUBLANES, 0), SUBLANES)
        hi = pl.multiple_of(jnp.minimum(r0 + R, S - SUBLANES), SUBLANES)
        prev = jnp.where(c > 0, x_ref[0, pl.ds(lo, SUBLANES), :], 0.0)
        nxt = jnp.where(c < n_chunks - 1, x_ref[0, pl.ds(hi, SUBLANES), :], 0.0)
        ext = jnp.concatenate([prev, main, nxt], axis=0)
        xc = cb
        for tap in range(CONV_W):
            o = SUBLANES - 2 + tap
            xc = xc + ext[o:o + R, :] * cw[tap:tap + 1, :]
        return r0, xc

    def direction(c, d, reverse):
        r0, xc = conv_chunk(c)
        xb = xc.astype(BF16)
        r = jax.nn.sigmoid(_dot(xb, gw_ref[d, 0, 0]) + gb_ref[d, 0])
        i = jax.nn.sigmoid(_dot(xb, gw_ref[d, 1, 0]) + gb_ref[d, 1])
        log_a = -RG_C * r * jax.nn.softplus(-lam_ref[d])
        a = jnp.exp(log_a)
        b = jnp.sqrt(jnp.tanh(-log_a) * (a * a + 1.0)) * (i * xc)
        h, carry = _scan_chunk(a, b, carry_ref[...], reverse)
        carry_ref[...] = carry
        return r0, h

    carry_ref[...] = jnp.zeros_like(carry_ref)

    def fwd_body(c, _):
        r0, h = direction(c, 0, False)
        hf_ref[pl.ds(r0, R), :] = h
        return 0

    lax.fori_loop(0, n_chunks, fwd_body, 0, unroll=2)
    carry_ref[...] = jnp.zeros_like(carry_ref)

    def bwd_body(j, _):
        c = n_chunks - 1 - j
        r0, h = direction(c, 1, True)
        gate = gate_ref[0, pl.ds(r0, R), :]
        y = jax.nn.gelu(gate) * (hf_ref[pl.ds(r0, R), :] + h)
        y_ref[0, pl.ds(r0, R), :] = y.astype(y_ref.dtype)
        return 0

    lax.fori_loop(0, n_chunks, bwd_body, 0, unroll=2)


def _rglru(rg, conv_w, conv_b, gw_bd, gate_b, lam):
    B, S, _ = rg.shape
    n_lg = D_RNN // LANES
    return pl.pallas_call(
        _rglru_kernel,
        grid=(B, n_lg),
        in_specs=[
            pl.BlockSpec((1, S, LANES), lambda b, j: (b, 0, n_lg + j)),
            pl.BlockSpec((1, S, LANES), lambda b, j: (b, 0, j)),
            pl.BlockSpec((CONV_W, LANES), lambda b, j: (0, j)),
            pl.BlockSpec((1, LANES), lambda b, j: (0, j)),
            pl.BlockSpec((2, 2, 1, LANES, LANES), lambda b, j: (0, 0, j, 0, 0)),
            pl.BlockSpec((2, 2, 1, LANES), lambda b, j: (0, 0, 0, j)),
            pl.BlockSpec((2, 1, LANES), lambda b, j: (0, 0, j)),
        ],
        out_specs=pl.BlockSpec((1, S, LANES), lambda b, j: (b, 0, j)),
        out_shape=jax.ShapeDtypeStruct((B, S, D_RNN), BF16),
        scratch_shapes=[pltpu.VMEM((S, LANES), F32), pltpu.VMEM((1, LANES), F32)],
        compiler_params=_cparams(("arbitrary", "arbitrary")),
        name="rglru",
    )(rg, rg, conv_w, conv_b, gw_bd, gate_b, lam)


def _attn_kernel(qt_ref, k_ref, vt_ref, o_ref, m_ref, acc_ref):
    S = k_ref.shape[2]
    G = ATT_HEADS // KV_HEADS
    m_ref[...] = jnp.full_like(m_ref, NEG_BIG)
    acc_ref[...] = jnp.zeros_like(acc_ref)

    def scores(t):
        j, g = divmod(t, G)
        return _dot(k_ref[0, 0, j * ATT_TK:(j + 1) * ATT_TK, :], qt_ref[0, g])

    n_items = (S // ATT_TK) * G
    pending = [scores(t) for t in range(min(ATT_LOOKAHEAD, n_items))]
    for t in range(n_items):
        j, g = divmod(t, G)
        if t + ATT_LOOKAHEAD < n_items:
            pending.append(scores(t + ATT_LOOKAHEAD))
        s = pending.pop(0)
        m_old = m_ref[g]
        m_new = jnp.maximum(m_old, jnp.max(s, axis=0, keepdims=True))
        p = jnp.exp2(s - m_new)
        pv = _dot(vt_ref[0, 0, :, j * ATT_TK:(j + 1) * ATT_TK], p.astype(BF16))
        acc_ref[g] = jnp.exp2(m_old - m_new) * acc_ref[g] + pv
        m_ref[g] = m_new
    for g in range(G):
        acc = acc_ref[g]
        o_ref[0, g] = (acc[:HEAD_DIM, :] / acc[HEAD_DIM:HEAD_DIM + 1, :]).astype(o_ref.dtype)


def _attention(q, k, v):
    B, S, _ = q.shape
    G = ATT_HEADS // KV_HEADS
    tq = ATT_TQ
    qt = q.reshape(B, S, ATT_HEADS, HEAD_DIM).transpose(0, 2, 3, 1)
    kh = k.reshape(B, S, KV_HEADS, HEAD_DIM).transpose(0, 2, 1, 3)
    vt = v.reshape(B, S, KV_HEADS, HEAD_DIM).transpose(0, 2, 3, 1)
    vt = jnp.concatenate([vt, jnp.ones((B, KV_HEADS, ATT_ONES_ROWS, S), v.dtype)], axis=2)
    vrows = HEAD_DIM + ATT_ONES_ROWS
    out_t = pl.pallas_call(
        _attn_kernel,
        grid=(B, KV_HEADS, S // tq),
        in_specs=[
            pl.BlockSpec((1, G, HEAD_DIM, tq), lambda b, h, i: (b, h, 0, i)),
            pl.BlockSpec((1, 1, S, HEAD_DIM), lambda b, h, i: (b, h, 0, 0)),
            pl.BlockSpec((1, 1, vrows, S), lambda b, h, i: (b, h, 0, 0)),
        ],
        out_specs=pl.BlockSpec((1, G, HEAD_DIM, tq), lambda b, h, i: (b, h, 0, i)),
        out_shape=jax.ShapeDtypeStruct((B, ATT_HEADS, HEAD_DIM, S), BF16),
        scratch_shapes=[
            pltpu.VMEM((G, 1, tq), F32),
            pltpu.VMEM((G, vrows, tq), F32),
        ],
        compiler_params=_cparams(("arbitrary", "arbitrary", "arbitrary")),
        name="attn",
    )(qt, kh, vt)
    return out_t.transpose(0, 3, 1, 2).reshape(B, S, D_ATT)


def _first_argmax(x, iota, size):
    m = jnp.max(x, axis=0, keepdims=True)
    return m, jnp.min(jnp.where(x == m, iota, size), axis=0, keepdims=True)


def _route_choices(hb, rwt, rb):
    tm = hb.shape[0]
    logits = lax.dot_general(rwt, hb, (((1,), (1,)), ((), ())), preferred_element_type=F32)
    scores = jax.nn.sigmoid(logits)
    biased = scores + rb
    iota_e = lax.broadcasted_iota(jnp.int32, (EPG, tm), 0).astype(F32)
    group_scores = []
    for gi in range(N_GROUPS):
        bg = biased[gi * EPG:(gi + 1) * EPG, :]
        m1, i1 = _first_argmax(bg, iota_e, EPG)
        m2 = jnp.max(jnp.where(iota_e == i1, -jnp.inf, bg), axis=0, keepdims=True)
        group_scores.append(m1 + m2)
    gs = jnp.concatenate(group_scores, axis=0)
    iota_g = lax.broadcasted_iota(jnp.int32, (N_GROUPS, tm), 0).astype(F32)
    _, g_sel = _first_argmax(gs, iota_g, N_GROUPS)
    in_b = jnp.zeros((EPG, tm), F32)
    in_s = jnp.zeros((EPG, tm), F32)
    for gi in range(N_GROUPS):
        sel = g_sel == gi
        in_b = jnp.where(sel, biased[gi * EPG:(gi + 1) * EPG, :], in_b)
        in_s = jnp.where(sel, scores[gi * EPG:(gi + 1) * EPG, :], in_s)
    _, l1 = _first_argmax(in_b, iota_e, EPG)
    _, l2 = _first_argmax(jnp.where(iota_e == l1, -jnp.inf, in_b), iota_e, EPG)
    w1 = jnp.sum(jnp.where(iota_e == l1, in_s, 0.0), axis=0, keepdims=True)
    w2 = jnp.sum(jnp.where(iota_e == l2, in_s, 0.0), axis=0, keepdims=True)
    wsum = w1 + w2
    experts = jnp.concatenate([g_sel * EPG + l1, g_sel * EPG + l2], axis=0)
    return experts, jnp.concatenate([w1 / wsum, w2 / wsum], axis=0)


def _outproj_route_kernel(*refs, n_in):
    y_refs = refs[:n_in]
    (w_ref, x_ref, gate_ref, g2_ref, sc2_ref, sh2_ref, rwt_ref, rb_ref, tri_ref,
     o_ref, idx_ref, wt_ref, rank_ref, cnt_ref, run_ref) = refs[n_in:]
    acc = None
    off = 0
    for yr in y_refs:
        kd = yr.shape[-1]
        part = _dot(yr[0], w_ref[off:off + kd, :])
        acc = part if acc is None else acc + part
        off += kd
    x_new = x_ref[0] + gate_ref[0] * acc
    o_ref[0] = x_new

    hb = _norm_mod(x_new, g2_ref[...], sc2_ref[0], sh2_ref[0]).astype(BF16)
    tm = hb.shape[0]
    experts, weights = _route_choices(hb, rwt_ref[...], rb_ref[...])
    idx_ref[...] = experts.astype(jnp.int32)
    wt_ref[...] = weights

    @pl.when((pl.program_id(0) == 0) & (pl.program_id(1) == 0))
    def _():
        run_ref[...] = jnp.zeros_like(run_ref)

    iota = lax.broadcasted_iota(jnp.int32, (N_EXPERTS, tm), 0).astype(F32)
    oh0 = jnp.where(iota == experts[0:1, :], 1.0, 0.0)
    oh1 = jnp.where(iota == experts[1:2, :], 1.0, 0.0)
    tri = tri_ref[...]
    pre0 = _dot(oh0.astype(BF16), tri)
    pre1 = _dot(oh1.astype(BF16), tri)
    tot0 = jnp.sum(oh0, axis=1, keepdims=True)
    tot1 = jnp.sum(oh1, axis=1, keepdims=True)
    run = run_ref[...]
    rank0 = jnp.sum(oh0 * (run + pre0), axis=0, keepdims=True)
    rank1 = jnp.sum(oh1 * (run + tot0 + pre1), axis=0, keepdims=True)
    rank_ref[...] = jnp.concatenate([rank0, rank1], axis=0).astype(jnp.int32)
    run = run + tot0 + tot1
    run_ref[...] = run
    cnt_ref[...] = run


def _outproj_route(ys, w, x, gate, g2, scale2, shift2, rwt, rb):
    B, S, D = x.shape
    tm = ROW_TILE
    n = S // tm
    row = lambda b, i: (b, i, 0)
    per_b = lambda b, i: (b, 0, 0)
    const2 = lambda b, i: (0, 0)
    tok = lambda b, i: (0, b * n + i)
    tri = (jnp.arange(tm)[:, None] < jnp.arange(tm)[None, :]).astype(BF16)
    in_specs = [pl.BlockSpec((1, tm, y.shape[-1]), row) for y in ys]
    in_specs += [
        pl.BlockSpec(w.shape, const2),
        pl.BlockSpec((1, tm, D), row),
        pl.BlockSpec((1, 1, D), per_b),
        pl.BlockSpec((1, D), const2),
        pl.BlockSpec((1, 1, D), per_b),
        pl.BlockSpec((1, 1, D), per_b),
        pl.BlockSpec((N_EXPERTS, D), const2),
        pl.BlockSpec((N_EXPERTS, 1), const2),
        pl.BlockSpec((tm, tm), const2),
    ]
    return pl.pallas_call(
        functools.partial(_outproj_route_kernel, n_in=len(ys)),
        grid=(B, n),
        in_specs=in_specs,
        out_specs=[
            pl.BlockSpec((1, tm, D), row),
            pl.BlockSpec((TOP_K, tm), tok),
            pl.BlockSpec((TOP_K, tm), tok),
            pl.BlockSpec((TOP_K, tm), tok),
            pl.BlockSpec((N_EXPERTS, 1), const2),
        ],
        out_shape=[
            jax.ShapeDtypeStruct((B, S, D), F32),
            jax.ShapeDtypeStruct((TOP_K, B * S), jnp.int32),
            jax.ShapeDtypeStruct((TOP_K, B * S), F32),
            jax.ShapeDtypeStruct((TOP_K, B * S), jnp.int32),
            jax.ShapeDtypeStruct((N_EXPERTS, 1), F32),
        ],
        scratch_shapes=[pltpu.VMEM((N_EXPERTS, 1), F32)],
        compiler_params=_cparams(("arbitrary", "arbitrary")),
        name="outproj_route",
    )(*ys, w, x, gate, g2, scale2, shift2, rwt, rb, tri)


def _inproj_ret_kernel(x_ref, g_ref, sc_ref, sh_ref, w_ref, cos_ref, sin_ref,
                       q_ref, k_ref, v_ref, gate_ref):
    h = _norm_mod(x_ref[0], g_ref[...], sc_ref[0], sh_ref[0]).astype(BF16)
    cos = cos_ref[...]
    sin = sin_ref[...]
    half = RET_QK // 2
    dqk = RET_HEADS * RET_QK
    dv = RET_HEADS * RET_V

    def rope(t):
        t1 = t[:, :half]
        t2 = t[:, half:]
        return jnp.concatenate([t1 * cos - t2 * sin, t2 * cos + t1 * sin], axis=-1)

    for hd in range(RET_HEADS):
        c = hd * RET_QK
        q_ref[0, :, c:c + RET_QK] = rope(_dot(h, w_ref[:, c:c + RET_QK])).astype(BF16)
        kh = rope(_dot(h, w_ref[:, dqk + c:dqk + c + RET_QK])) * (RET_QK ** -0.5)
        k_ref[0, :, c:c + RET_QK] = kh.astype(BF16)
    for hd in range(RET_HEADS):
        c = hd * RET_V
        v_ref[0, :, c:c + RET_V] = _dot(h, w_ref[:, 2 * dqk + c:2 * dqk + c + RET_V]).astype(BF16)
        gate_ref[0, :, c:c + RET_V] = _dot(h, w_ref[:, 2 * dqk + dv + c:2 * dqk + dv + c + RET_V])


def _inproj_ret(x, g, scale, shift, w, cos_t, sin_t):
    B, S, D = x.shape
    tm = ROW_TILE
    N = w.shape[1]
    dqk = RET_HEADS * RET_QK
    dv = RET_HEADS * RET_V
    half = RET_QK // 2
    row = lambda b, i: (b, i, 0)
    per_b = lambda b, i: (b, 0, 0)
    const2 = lambda b, i: (0, 0)
    return pl.pallas_call(
        _inproj_ret_kernel,
        grid=(B, S // tm),
        in_specs=[
            pl.BlockSpec((1, tm, D), row),
            pl.BlockSpec((1, D), const2),
            pl.BlockSpec((1, 1, D), per_b),
            pl.BlockSpec((1, 1, D), per_b),
            pl.BlockSpec((D, N), const2),
            pl.BlockSpec((tm, half), lambda b, i: (i, 0)),
            pl.BlockSpec((tm, half), lambda b, i: (i, 0)),
        ],
        out_specs=[
            pl.BlockSpec((1, tm, dqk), row),
            pl.BlockSpec((1, tm, dqk), row),
            pl.BlockSpec((1, tm, dv), row),
            pl.BlockSpec((1, tm, dv), row),
        ],
        out_shape=[
            jax.ShapeDtypeStruct((B, S, dqk), BF16),
            jax.ShapeDtypeStruct((B, S, dqk), BF16),
            jax.ShapeDtypeStruct((B, S, dv), BF16),
            jax.ShapeDtypeStruct((B, S, dv), F32),
        ],
        compiler_params=_cparams(("arbitrary", "arbitrary")),
        name="inproj_ret",
    )(x, g, scale, shift, w, cos_t, sin_t)


def _ret_kernel(*refs, reverse):
    if reverse:
        q_ref, kt_ref, v_ref, dm_ref, qd_ref, kd_ref, cd_ref, yf_ref, g_ref, o_ref, st_ref = refs
    else:
        q_ref, kt_ref, v_ref, dm_ref, qd_ref, kd_ref, cd_ref, o_ref, st_ref = refs
    C = RET_CHUNK
    n_sub = q_ref.shape[1] // C

    @pl.when(pl.program_id(2) == 0)
    def _():
        st_ref[...] = jnp.zeros_like(st_ref)

    dm = dm_ref[0]
    qd = qd_ref[0]
    kd = kd_ref[0]
    cd = cd_ref[0]
    order = range(n_sub - 1, -1, -1) if reverse else range(n_sub)
    for ci in order:
        r = slice(ci * C, (ci + 1) * C)
        qc = q_ref[0, r, :]
        ktc = kt_ref[0, 0, :, r]
        vc = v_ref[0, r, :]
        st = st_ref[...]
        scores = _dot(qc, ktc) * dm
        out = _dot(scores.astype(BF16), vc) + _dot(qc, st.astype(BF16)) * qd
        st_ref[...] = st * cd + _dot((ktc.astype(F32) * kd).astype(BF16), vc)
        if reverse:
            y = yf_ref[0, r, :] + out
            y = y * lax.rsqrt(jnp.mean(y * y, axis=-1, keepdims=True) + NORM_EPS)
            o_ref[0, r, :] = (y * _silu(g_ref[0, r, :])).astype(o_ref.dtype)
        else:
            o_ref[0, r, :] = out


def _retention_pass(q, kt, v, tabs, reverse, yf=None, gate=None):
    B, S, _ = q.shape
    ts = min(RET_STEP_ROWS, S)
    n = S // ts
    C = RET_CHUNK
    step = (lambda i: n - 1 - i) if reverse else (lambda i: i)
    rows = lambda width: pl.BlockSpec((1, ts, width), lambda b, h, i: (b, step(i), h))
    per_h = lambda shape: pl.BlockSpec((1,) + shape, lambda b, h, i: (h, 0, 0))
    in_specs = [
        rows(RET_QK),
        pl.BlockSpec((1, 1, RET_QK, ts), lambda b, h, i: (b, h, 0, step(i))),
        rows(RET_V),
        per_h((C, C)), per_h((C, 1)), per_h((1, C)), per_h((1, 1)),
    ]
    args = [q, kt, v, *tabs]
    if reverse:
        in_specs += [rows(RET_V), rows(RET_V)]
        args += [yf, gate]
    return pl.pallas_call(
        functools.partial(_ret_kernel, reverse=reverse),
        grid=(B, RET_HEADS, n),
        in_specs=in_specs,
        out_specs=rows(RET_V),
        out_shape=jax.ShapeDtypeStruct((B, S, RET_HEADS * RET_V), BF16 if reverse else F32),
        scratch_shapes=[pltpu.VMEM((RET_QK, RET_V), F32)],
        compiler_params=_cparams(("arbitrary", "arbitrary", "arbitrary")),
        name="retention_bwd" if reverse else "retention_fwd",
    )(*args)


def _retention_tables(offset, strict):
    C = RET_CHUNK
    log_gamma = jnp.log1p(-jnp.exp2(-5.0 - offset - jnp.arange(RET_HEADS, dtype=F32)))
    idx = jnp.arange(C, dtype=F32)
    diff = idx[:, None] - idx[None, :]
    mask = (diff > 0) if strict else (diff >= 0)
    d_intra = jnp.where(mask, jnp.exp(jnp.where(mask, diff, 0.0)[None] * log_gamma[:, None, None]), 0.0)
    q_dec = jnp.exp((idx + 1.0)[None, :] * log_gamma[:, None])
    k_dec = jnp.exp((C - 1.0 - idx)[None, :] * log_gamma[:, None])
    chunk_dec = jnp.exp(C * log_gamma)
    return d_intra, q_dec, k_dec, chunk_dec


def _shape_tables(d_intra, q_dec, k_dec, chunk_dec, flip):
    if flip:
        d_intra = d_intra[:, ::-1, ::-1]
        q_dec = q_dec[:, ::-1]
        k_dec = k_dec[:, ::-1]
    return (d_intra, q_dec[:, :, None], k_dec[:, None, :], chunk_dec[:, None, None])


def _dispatch_plan(idx, rank, counts):
    T = idx.shape[1]
    A = T * TOP_K
    bm = MOE_BM
    nb = (A + N_EXPERTS * bm) // bm
    counts = counts.reshape(N_EXPERTS).astype(jnp.int32)
    padded = ((counts + bm - 1) // bm) * bm
    pad_end = jnp.cumsum(padded)
    pad_start = pad_end - padded
    experts = jnp.arange(N_EXPERTS, dtype=jnp.int32)
    dest = jnp.sum(jnp.where(idx[..., None] == experts, pad_start, 0), axis=-1) + rank
    block_row0 = jnp.arange(nb, dtype=jnp.int32) * bm
    block_expert = jnp.minimum(
        jnp.sum((pad_end[None, :] <= block_row0[:, None]).astype(jnp.int32), axis=1), N_EXPERTS - 1)
    n_used = (pad_end[-1] // bm).astype(jnp.int32).reshape(1)
    tail_block = jnp.maximum(pad_end // bm - 1, 0).astype(jnp.int32)
    tm = COMB_TM
    dest_tiles = dest.reshape(TOP_K, T // tm, tm).transpose(1, 0, 2).reshape(T // tm, 1, TOP_K * tm)
    return block_expert, n_used, tail_block, dest_tiles


def _zero_block_kernel(tail_ref, o_ref):
    del tail_ref
    o_ref[...] = jnp.zeros_like(o_ref)


def _zero_tail_blocks(tail_block, n_rows, width):
    grid_spec = pltpu.PrefetchScalarGridSpec(
        num_scalar_prefetch=1,
        grid=(N_EXPERTS,),
        in_specs=[],
        out_specs=pl.BlockSpec((MOE_BM, width), lambda e, tail: (tail[e], 0)),
    )
    return pl.pallas_call(
        _zero_block_kernel,
        grid_spec=grid_spec,
        out_shape=jax.ShapeDtypeStruct((n_rows, width), F32),
        compiler_params=_cparams(("arbitrary",)),
        name="zero_tail",
    )(tail_block)


def _row_copy(src, src_row, dst, dst_row, sem):
    return pltpu.make_async_copy(src.at[pl.ds(src_row, 1), :], dst.at[pl.ds(dst_row, 1), :], sem)


def _dispatch_kernel(dest_ref, x_ref, g_ref, sc_ref, sh_ref, gate_ref, sg_ref, su_ref, sd_ref, xd_in_ref,
                     xb_ref, xd_ref, hbuf, sem, *, n_steps):
    del xd_in_ref
    step = pl.program_id(0) * pl.num_programs(1) + pl.program_id(1)
    slot = step % 2
    x = x_ref[0]
    tm = x.shape[0]

    def drain(s):
        for _ in range(TOP_K * tm):
            _row_copy(hbuf.at[s], 0, xd_ref, 0, sem.at[s]).wait()

    @pl.when(step >= 2)
    def _():
        drain(slot)

    h = _norm_mod(x, g_ref[...], sc_ref[0], sh_ref[0])
    hbuf[slot] = h
    for r in range(tm):
        for k in range(TOP_K):
            _row_copy(hbuf.at[slot], r, xd_ref, dest_ref[0, 0, k * tm + r], sem.at[slot]).start()
    hb = h.astype(BF16)
    act = _silu(_dot(hb, sg_ref[...])) * _dot(hb, su_ref[...])
    xb_ref[0] = x + gate_ref[0] * _dot(act.astype(BF16), sd_ref[...])

    @pl.when(step == n_steps - 1)
    def _():
        drain(slot)
        if n_steps > 1:
            drain(1 - slot)


def _dispatch_shared(dest_tiles, x, g, scale, shift, gate, sg, su, sd, x_disp):
    B, S, D = x.shape
    tm = COMB_TM
    n = S // tm
    row = lambda b, i: (b, i, 0)
    per_b = lambda b, i: (b, 0, 0)
    const2 = lambda b, i: (0, 0)
    return pl.pallas_call(
        functools.partial(_dispatch_kernel, n_steps=B * n),
        grid=(B, n),
        in_specs=[
            pl.BlockSpec((1, 1, TOP_K * tm), lambda b, i: (b * n + i, 0, 0), memory_space=pltpu.SMEM),
            pl.BlockSpec((1, tm, D), row),
            pl.BlockSpec((1, D), const2),
            pl.BlockSpec((1, 1, D), per_b),
            pl.BlockSpec((1, 1, D), per_b),
            pl.BlockSpec((1, 1, D), per_b),
            pl.BlockSpec((D, D_FF), const2),
            pl.BlockSpec((D, D_FF), const2),
            pl.BlockSpec((D_FF, D), const2),
            pl.BlockSpec(memory_space=pl.ANY),
        ],
        out_specs=[pl.BlockSpec((1, tm, D), row), pl.BlockSpec(memory_space=pl.ANY)],
        out_shape=[jax.ShapeDtypeStruct((B, S, D), F32), jax.ShapeDtypeStruct(x_disp.shape, F32)],
        input_output_aliases={9: 1},
        scratch_shapes=[pltpu.VMEM((2, tm, D), F32), pltpu.SemaphoreType.DMA((2,))],
        compiler_params=_cparams(("arbitrary", "arbitrary")),
        name="moe_dispatch",
    )(dest_tiles, x, g, scale, shift, gate, sg, su, sd, x_disp)


def _expert_kernel(be_ref, nu_ref, x_ref, wg_ref, wu_ref, wd_ref, o_ref):
    del be_ref

    @pl.when(pl.program_id(0) < nu_ref[0])
    def _():
        xb = x_ref[...].astype(BF16)
        act = _silu(_dot(xb, wg_ref[0])) * _dot(xb, wu_ref[0])
        o_ref[...] = _dot(act.astype(BF16), wd_ref[0])

    @pl.when(pl.program_id(0) >= nu_ref[0])
    def _():
        o_ref[...] = jnp.zeros_like(o_ref)


def _experts(block_expert, n_used, x_disp, wg, wu, wd):
    P, D = x_disp.shape
    bm = MOE_BM
    nb = P // bm
    blk = lambda i, be, nu: (jnp.minimum(i, jnp.maximum(nu[0] - 1, 0)), 0)
    wsel = lambda i, be, nu: (be[jnp.minimum(i, jnp.maximum(nu[0] - 1, 0))], 0, 0)
    grid_spec = pltpu.PrefetchScalarGridSpec(
        num_scalar_prefetch=2,
        grid=(nb,),
        in_specs=[
            pl.BlockSpec((bm, D), blk),
            pl.BlockSpec((1, D, D_FF), wsel),
            pl.BlockSpec((1, D, D_FF), wsel),
            pl.BlockSpec((1, D_FF, D), wsel),
        ],
        out_specs=pl.BlockSpec((bm, D), lambda i, be, nu: (i, 0)),
    )
    return pl.pallas_call(
        _expert_kernel,
        grid_spec=grid_spec,
        out_shape=jax.ShapeDtypeStruct((P, D), F32),
        compiler_params=_cparams(("arbitrary",)),
        name="experts",
    )(block_expert, n_used, x_disp, wg, wu, wd)


def _combine_kernel(pos_cur_ref, pos_next_ref, y_hbm, w_ref, xb_ref, gate_ref, o_ref, ybuf, sem):
    i = pl.program_id(0)
    n = pl.num_programs(0)
    slot = i % 2
    rows = ybuf.shape[1]
    tm = rows // TOP_K

    def issue(pos_ref, dst_slot):
        for r in range(rows):
            _row_copy(y_hbm, pos_ref[0, 0, r], ybuf.at[dst_slot], r, sem.at[dst_slot]).start()

    @pl.when(i == 0)
    def _():
        issue(pos_cur_ref, 0)

    @pl.when(i + 1 < n)
    def _():
        issue(pos_next_ref, 1 - slot)

    for _ in range(rows):
        _row_copy(y_hbm, 0, ybuf.at[slot], 0, sem.at[slot]).wait()
    w = w_ref[...]
    y = ybuf[slot, 0:tm, :] * w[:, 0:1] + ybuf[slot, tm:rows, :] * w[:, 1:2]
    o_ref[...] = xb_ref[...] + gate_ref[0] * y


def _combine(pos_tiles, y_disp, w_cols, xb, gate, tiles_per_seq):
    T, D = xb.shape
    tm = COMB_TM
    n = T // tm
    smem_blk = lambda f: pl.BlockSpec((1, 1, TOP_K * tm), f, memory_space=pltpu.SMEM)
    return pl.pallas_call(
        _combine_kernel,
        grid=(n,),
        in_specs=[
            smem_blk(lambda i: (i, 0, 0)),
            smem_blk(lambda i: (jnp.minimum(i + 1, n - 1), 0, 0)),
            pl.BlockSpec(memory_space=pl.ANY),
            pl.BlockSpec((tm, TOP_K), lambda i: (i, 0)),
            pl.BlockSpec((tm, D), lambda i: (i, 0)),
            pl.BlockSpec((1, 1, D), lambda i: (i // tiles_per_seq, 0, 0)),
        ],
        out_specs=pl.BlockSpec((tm, D), lambda i: (i, 0)),
        out_shape=jax.ShapeDtypeStruct((T, D), F32),
        scratch_shapes=[pltpu.VMEM((2, TOP_K * tm, D), F32), pltpu.SemaphoreType.DMA((2,))],
        compiler_params=_cparams(("arbitrary",)),
        name="combine",
    )(pos_tiles, pos_tiles, y_disp, w_cols, xb, gate)


def _axial_angles(S, hd):
    rows = S // GRID_W
    row = jnp.repeat(jnp.arange(rows), GRID_W).astype(F32)
    col = jnp.tile(jnp.arange(GRID_W), rows).astype(F32)
    n = hd // 4
    inv = ROPE_THETA ** (-jnp.arange(n, dtype=F32) / n)
    ang = jnp.concatenate([row[:, None] * inv, col[:, None] * inv], axis=-1)
    return jnp.cos(ang), jnp.sin(ang)


def _block_diag_gates(gate_w):
    per = LANES // RG_BW
    n_lg = D_RNN // LANES
    g = gate_w.reshape(2, 2, n_lg, per, RG_BW, RG_BW)
    eye = jnp.eye(per, dtype=gate_w.dtype)
    bd = jnp.einsum('dglpij,pq->dglpiqj', g, eye)
    return bd.reshape(2, 2, n_lg, LANES, LANES).astype(BF16)


def kernel(x_prompt, x_sample, c_prompt, c_sample, norm_mix, norm_ffn, w_ada, b_ada, w_in_ab, conv_w,
           conv_b, rg_gate_w, rg_gate_b, rg_lambda, q_norm, k_norm, w_out_ab, w_in_ret, w_out_ret,
           router_w, router_bias, exp_w_gate, exp_w_up, exp_w_down, sh_w_gate, sh_w_up, sh_w_down):
    assert x_prompt.shape[1:] == x_sample.shape[1:]
    Bp_, S, D = x_prompt.shape
    B = Bp_ + x_sample.shape[0]
    T = B * S
    x = jnp.concatenate([x_prompt, x_sample], axis=0)
    c = jnp.concatenate([c_prompt, c_sample], axis=0)
    c_rows = -(-B // 16) * 16
    c_pad = jnp.zeros((c_rows, D), F32).at[:B].set(c)
    mod = _ada_mod(c_pad, w_ada, b_ada)[:, :B].reshape(DEPTH, B, 6, 1, D)

    cos_a, sin_a = _axial_angles(S, HEAD_DIM)
    cos_att = jnp.tile(jnp.concatenate([cos_a, cos_a], axis=-1), (1, ATT_HEADS))
    sin_att = jnp.tile(jnp.concatenate([-sin_a, sin_a], axis=-1), (1, ATT_HEADS))
    cos_r, sin_r = _axial_angles(S, RET_QK)
    seg = jnp.arange(D_ATT) // HEAD_DIM
    ones_bd = (seg[:, None] == seg[None, :]).astype(BF16)
    tabs_f = _shape_tables(*_retention_tables(0.0, False), flip=False)
    tabs_b = _shape_tables(*_retention_tables(0.5, True), flip=True)
    rwt = router_w.T.astype(BF16)
    rb = router_bias.reshape(N_EXPERTS, 1).astype(F32)

    for l in range(DEPTH):
        shift1, scale1, gate1, shift2, scale2, gate2 = (mod[l, :, j] for j in range(6))
        g_mix = norm_mix[l].reshape(1, D)
        if l % 2 == 0:
            e = l // 2
            rg, q, k, v = _inproj_even(
                x, g_mix, scale1, shift1, w_in_ab[e].astype(BF16), ones_bd,
                jnp.tile(q_norm[e], ATT_HEADS).reshape(1, D_ATT),
                jnp.tile(k_norm[e], KV_HEADS).reshape(1, KV_HEADS * HEAD_DIM), cos_att, sin_att)
            y_rg = _rglru(rg, conv_w[e], conv_b[e].reshape(1, D_RNN), _block_diag_gates(rg_gate_w[e]),
                          rg_gate_b[e].reshape(2, 2, 1, D_RNN), rg_lambda[e].reshape(2, 1, D_RNN))
            y_att = _attention(q, k, v)
            ys, w_out = [y_rg, y_att], w_out_ab[e]
        else:
            o = l // 2
            q, k, v, gate = _inproj_ret(x, g_mix, scale1, shift1, w_in_ret[o].astype(BF16), cos_r, sin_r)
            kt = k.reshape(B, S, RET_HEADS, RET_QK).transpose(0, 2, 3, 1)
            y_f = _retention_pass(q, kt, v, tabs_f, reverse=False)
            y_n = _retention_pass(q, kt, v, tabs_b, reverse=True, yf=y_f, gate=gate)
            ys, w_out = [y_n], w_out_ret[o]

        g_ffn = norm_ffn[l].reshape(1, D)
        x, idx, wts, rank, counts = _outproj_route(
            ys, w_out.astype(BF16), x, gate1, g_ffn, scale2, shift2, rwt, rb)
        block_expert, n_used, tail_block, dest_tiles = _dispatch_plan(idx, rank, counts)
        x_disp = _zero_tail_blocks(tail_block, T * TOP_K + N_EXPERTS * MOE_BM, D)
        xb, x_disp = _dispatch_shared(
            dest_tiles, x, g_ffn, scale2, shift2, gate2,
            sh_w_gate[l].astype(BF16), sh_w_up[l].astype(BF16), sh_w_down[l].astype(BF16), x_disp)
        y_disp = _experts(block_expert, n_used, x_disp,
                          exp_w_gate[l].astype(BF16), exp_w_up[l].astype(BF16), exp_w_down[l].astype(BF16))
        x = _combine(dest_tiles, y_disp, wts.T, xb.reshape(T, D), gate2, S // COMB_TM).reshape(B, S, D)

    return x[:Bp_], x[Bp_:]
```

```python
import functools

import jax
import jax.numpy as jnp
from jax import lax
from jax.experimental import pallas as pl
from jax.experimental.pallas import tpu as pltpu

F32 = jnp.float32
BF16 = jnp.bfloat16

D_MODEL = 1024
DEPTH = 4
GRID_W = 64
NORM_EPS = 1e-6
ROPE_THETA = 10000.0
D_RNN = 512
RG_BW = 64
CONV_W = 4
RG_C = 8.0
ATT_HEADS = 8
KV_HEADS = 2
HEAD_DIM = 64
D_ATT = ATT_HEADS * HEAD_DIM
RET_HEADS = 4
RET_QK = 256
RET_V = 512
RET_CHUNK = 128
N_EXPERTS = 32
N_GROUPS = 4
EPG = N_EXPERTS // N_GROUPS
TOP_K = 2
D_FF = 512

LANES = 128
SUBLANES = 8
VMEM_LIMIT = 52 * 1024 * 1024

ROW_TILE = 256
ATT_TQ = 256
ATT_TK = 256
ATT_LOOKAHEAD = 6
ATT_ONES_ROWS = 16
SCAN_ROWS = 128
RET_STEP_ROWS = 2048
MOE_BM = 512
COMB_TM = 256
NEG_BIG = -1e30
ATT_Q_SCALE = HEAD_DIM ** -0.5 * 1.4426950408889634


def _cparams(sem):
    return pltpu.CompilerParams(dimension_semantics=sem, vmem_limit_bytes=VMEM_LIMIT)


def _dot(a, b):
    return jnp.dot(a, b, preferred_element_type=F32)


def _silu(x):
    return x * jax.nn.sigmoid(x)


def _norm_mod(x, g, scale, shift):
    ms = jnp.mean(x * x, axis=-1, keepdims=True)
    h = x * lax.rsqrt(ms + NORM_EPS) * g
    return h * (1.0 + scale) + shift


def _ada_kernel(c_ref, w_ref, b_ref, o_ref):
    c = c_ref[...]
    o_ref[0] = _dot(_silu(c).astype(BF16), w_ref[0].astype(BF16)) + b_ref[0]


def _ada_mod(c_pad, w_ada, b_ada):
    L, D, N = w_ada.shape
    Bp = c_pad.shape[0]
    tn = D_MODEL
    return pl.pallas_call(
        _ada_kernel,
        grid=(L, N // tn),
        in_specs=[
            pl.BlockSpec((Bp, D), lambda l, j: (0, 0)),
            pl.BlockSpec((1, D, tn), lambda l, j: (l, 0, j)),
            pl.BlockSpec((1, 1, tn), lambda l, j: (l, 0, j)),
        ],
        out_specs=pl.BlockSpec((1, Bp, tn), lambda l, j: (l, 0, j)),
        out_shape=jax.ShapeDtypeStruct((L, Bp, N), F32),
        compiler_params=_cparams(("arbitrary", "arbitrary")),
        name="ada",
    )(c_pad, w_ada, b_ada.reshape(L, 1, N))


def _segment_mean_sq(x, ones_bd, width):
    sq = x * x
    hi = sq.astype(BF16)
    lo = (sq - hi.astype(F32)).astype(BF16)
    return (_dot(hi, ones_bd) + _dot(lo, ones_bd)) * (1.0 / width)


def _rope_halves(x, cos2, sin_signed, half):
    w = x.shape[-1]
    lane = lax.broadcasted_iota(jnp.int32, x.shape, 1)
    first = (lane % (2 * half)) < half
    rot = jnp.where(first, pltpu.roll(x, w - half, 1), pltpu.roll(x, half, 1))
    return x * cos2 + rot * sin_signed


def _inproj_even_kernel(x_ref, g_ref, sc_ref, sh_ref, w_ref, ones_ref, qn_ref, kn_ref,
                        cos_ref, sin_ref, rg_ref, q_ref, k_ref, v_ref):
    h = _norm_mod(x_ref[0], g_ref[...], sc_ref[0], sh_ref[0]).astype(BF16)
    rg_ref[0] = _dot(h, w_ref[:, 0:2 * D_RNN])
    c0 = 2 * D_RNN
    q = _dot(h, w_ref[:, c0:c0 + D_ATT])
    c1 = c0 + D_ATT
    kw = KV_HEADS * HEAD_DIM
    k = _dot(h, w_ref[:, c1:c1 + kw])
    v_ref[0] = _dot(h, w_ref[:, c1 + kw:c1 + 2 * kw]).astype(BF16)

    half = HEAD_DIM // 2
    ones = ones_ref[...]
    qn = q * lax.rsqrt(_segment_mean_sq(q, ones, HEAD_DIM) + NORM_EPS) * qn_ref[...]
    qr = _rope_halves(qn, cos_ref[...], sin_ref[...], half)
    q_ref[0] = (qr * ATT_Q_SCALE).astype(BF16)
    kn = k * lax.rsqrt(_segment_mean_sq(k, ones[:kw, :kw], HEAD_DIM) + NORM_EPS) * kn_ref[...]
    kr = _rope_halves(kn, cos_ref[:, :kw], sin_ref[:, :kw], half)
    k_ref[0] = kr.astype(BF16)


def _inproj_even(x, g, scale, shift, w, ones_bd, qn, kn, cos_t, sin_t):
    B, S, D = x.shape
    tm = ROW_TILE
    N = w.shape[1]
    kw = KV_HEADS * HEAD_DIM
    row = lambda b, i: (b, i, 0)
    per_b = lambda b, i: (b, 0, 0)
    const2 = lambda b, i: (0, 0)
    return pl.pallas_call(
        _inproj_even_kernel,
        grid=(B, S // tm),
        in_specs=[
            pl.BlockSpec((1, tm, D), row),
            pl.BlockSpec((1, D), const2),
            pl.BlockSpec((1, 1, D), per_b),
            pl.BlockSpec((1, 1, D), per_b),
            pl.BlockSpec((D, N), const2),
            pl.BlockSpec((D_ATT, D_ATT), const2),
            pl.BlockSpec((1, D_ATT), const2),
            pl.BlockSpec((1, kw), const2),
            pl.BlockSpec((tm, D_ATT), lambda b, i: (i, 0)),
            pl.BlockSpec((tm, D_ATT), lambda b, i: (i, 0)),
        ],
        out_specs=[
            pl.BlockSpec((1, tm, 2 * D_RNN), row),
            pl.BlockSpec((1, tm, D_ATT), row),
            pl.BlockSpec((1, tm, kw), row),
            pl.BlockSpec((1, tm, kw), row),
        ],
        out_shape=[
            jax.ShapeDtypeStruct((B, S, 2 * D_RNN), F32),
            jax.ShapeDtypeStruct((B, S, D_ATT), BF16),
            jax.ShapeDtypeStruct((B, S, kw), BF16),
            jax.ShapeDtypeStruct((B, S, kw), BF16),
        ],
        compiler_params=_cparams(("arbitrary", "arbitrary")),
        name="inproj_even",
    )(x, g, scale, shift, w, ones_bd, qn, kn, cos_t, sin_t)


def _scan_chunk(a, b, carry, reverse):
    R = a.shape[0]
    row = lax.broadcasted_iota(jnp.int32, a.shape, 0) % SUBLANES
    s = 1
    while s < SUBLANES:
        if reverse:
            keep = row < SUBLANES - s
            shift = R - s
        else:
            keep = row >= s
            shift = s
        a_sh = jnp.where(keep, pltpu.roll(a, shift, 0), 1.0)
        b_sh = jnp.where(keep, pltpu.roll(b, shift, 0), 0.0)
        b = a * b_sh + b
        a = a * a_sh
        s *= 2
    n_groups = R // SUBLANES
    hs = [None] * n_groups
    for gi in (range(n_groups - 1, -1, -1) if reverse else range(n_groups)):
        rows = slice(gi * SUBLANES, (gi + 1) * SUBLANES)
        h = a[rows] * carry + b[rows]
        carry = h[0:1, :] if reverse else h[SUBLANES - 1:SUBLANES, :]
        hs[gi] = h
    return jnp.concatenate(hs, axis=0), carry


def _rglru_kernel(x_ref, gate_ref, cw_ref, cb_ref, gw_ref, gb_ref, lam_ref, y_ref, hf_ref, carry_ref):
    S = x_ref.shape[1]
    R = SCAN_ROWS
    n_chunks = S // R
    cw = cw_ref[...]
    cb = cb_ref[...]

    def conv_chunk(c):
        r0 = pl.multiple_of(c * R, R)
        main = x_ref[0, pl.ds(r0, R), :]
        lo = pl.multiple_of(jnp.maximum(r0 - SUBLANES, 0), SUBLANES)
        hi = pl.multiple_of(jnp.minimum(r0 + R, S - SUBLANES), SUBLANES)
        prev = jnp.where(c > 0, x_ref[0, pl.ds(lo, SUBLANES), :], 0.0)
        nxt = jnp.where(c < n_chunks - 1, x_ref[0, pl.ds(hi, SUBLANES), :], 0.0)
        ext = jnp.concatenate([prev, main, nxt], axis=0)
        xc = cb
        for tap in range(CONV_W):
            o = SUBLANES - 2 + tap
            xc = xc + ext[o:o + R, :] * cw[tap:tap + 1, :]
        return r0, xc

    def direction(c, d, reverse):
        r0, xc = conv_chunk(c)
        xb = xc.astype(BF16)
        r = jax.nn.sigmoid(_dot(xb, gw_ref[d, 0, 0]) + gb_ref[d, 0])
        i = jax.nn.sigmoid(_dot(xb, gw_ref[d, 1, 0]) + gb_ref[d, 1])
        log_a = -RG_C * r * jax.nn.softplus(-lam_ref[d])
        a = jnp.exp(log_a)
        b = jnp.sqrt(jnp.tanh(-log_a) * (a * a + 1.0)) * (i * xc)
        h, carry = _scan_chunk(a, b, carry_ref[...], reverse)
        carry_ref[...] = carry
        return r0, h

    carry_ref[...] = jnp.zeros_like(carry_ref)

    def fwd_body(c, _):
        r0, h = direction(c, 0, False)
        hf_ref[pl.ds(r0, R), :] = h
        return 0

    lax.fori_loop(0, n_chunks, fwd_body, 0, unroll=2)
    carry_ref[...] = jnp.zeros_like(carry_ref)

    def bwd_body(j, _):
        c = n_chunks - 1 - j
        r0, h = direction(c, 1, True)
        gate = gate_ref[0, pl.ds(r0, R), :]
        y = jax.nn.gelu(gate) * (hf_ref[pl.ds(r0, R), :] + h)
        y_ref[0, pl.ds(r0, R), :] = y.astype(y_ref.dtype)
        return 0

    lax.fori_loop(0, n_chunks, bwd_body, 0, unroll=2)


def _rglru(rg, conv_w, conv_b, gw_bd, gate_b, lam):
    B, S, _ = rg.shape
    n_lg = D_RNN // LANES
    return pl.pallas_call(
        _rglru_kernel,
        grid=(B, n_lg),
        in_specs=[
            pl.BlockSpec((1, S, LANES), lambda b, j: (b, 0, n_lg + j)),
            pl.BlockSpec((1, S, LANES), lambda b, j: (b, 0, j)),
            pl.BlockSpec((CONV_W, LANES), lambda b, j: (0, j)),
            pl.BlockSpec((1, LANES), lambda b, j: (0, j)),
            pl.BlockSpec((2, 2, 1, LANES, LANES), lambda b, j: (0, 0, j, 0, 0)),
            pl.BlockSpec((2, 2, 1, LANES), lambda b, j: (0, 0, 0, j)),
            pl.BlockSpec((2, 1, LANES), lambda b, j: (0, 0, j)),
        ],
        out_specs=pl.BlockSpec((1, S, LANES), lambda b, j: (b, 0, j)),
        out_shape=jax.ShapeDtypeStruct((B, S, D_RNN), BF16),
        scratch_shapes=[pltpu.VMEM((S, LANES), F32), pltpu.VMEM((1, LANES), F32)],
        compiler_params=_cparams(("arbitrary", "arbitrary")),
        name="rglru",
    )(rg, rg, conv_w, conv_b, gw_bd, gate_b, lam)


def _attn_kernel(qt_ref, k_ref, vt_ref, o_ref, m_ref, acc_ref):
    S = k_ref.shape[2]
    G = ATT_HEADS // KV_HEADS
    m_ref[...] = jnp.full_like(m_ref, NEG_BIG)
    acc_ref[...] = jnp.zeros_like(acc_ref)

    def scores(t):
        j, g = divmod(t, G)
        return _dot(k_ref[0, 0, j * ATT_TK:(j + 1) * ATT_TK, :], qt_ref[0, g])

    n_items = (S // ATT_TK) * G
    pending = [scores(t) for t in range(min(ATT_LOOKAHEAD, n_items))]
    for t in range(n_items):
        j, g = divmod(t, G)
        if t + ATT_LOOKAHEAD < n_items:
            pending.append(scores(t + ATT_LOOKAHEAD))
        s = pending.pop(0)
        m_old = m_ref[g]
        m_new = jnp.maximum(m_old, jnp.max(s, axis=0, keepdims=True))
        p = jnp.exp2(s - m_new)
        pv = _dot(vt_ref[0, 0, :, j * ATT_TK:(j + 1) * ATT_TK], p.astype(BF16))
        acc_ref[g] = jnp.exp2(m_old - m_new) * acc_ref[g] + pv
        m_ref[g] = m_new
    for g in range(G):
        acc = acc_ref[g]
        o_ref[0, g] = (acc[:HEAD_DIM, :] / acc[HEAD_DIM:HEAD_DIM + 1, :]).astype(o_ref.dtype)


def _attention(q, k, v):
    B, S, _ = q.shape
    G = ATT_HEADS // KV_HEADS
    tq = ATT_TQ
    qt = q.reshape(B, S, ATT_HEADS, HEAD_DIM).transpose(0, 2, 3, 1)
    kh = k.reshape(B, S, KV_HEADS, HEAD_DIM).transpose(0, 2, 1, 3)
    vt = v.reshape(B, S, KV_HEADS, HEAD_DIM).transpose(0, 2, 3, 1)
    vt = jnp.concatenate([vt, jnp.ones((B, KV_HEADS, ATT_ONES_ROWS, S), v.dtype)], axis=2)
    vrows = HEAD_DIM + ATT_ONES_ROWS
    out_t = pl.pallas_call(
        _attn_kernel,
        grid=(B, KV_HEADS, S // tq),
        in_specs=[
            pl.BlockSpec((1, G, HEAD_DIM, tq), lambda b, h, i: (b, h, 0, i)),
            pl.BlockSpec((1, 1, S, HEAD_DIM), lambda b, h, i: (b, h, 0, 0)),
            pl.BlockSpec((1, 1, vrows, S), lambda b, h, i: (b, h, 0, 0)),
        ],
        out_specs=pl.BlockSpec((1, G, HEAD_DIM, tq), lambda b, h, i: (b, h, 0, i)),
        out_shape=jax.ShapeDtypeStruct((B, ATT_HEADS, HEAD_DIM, S), BF16),
        scratch_shapes=[
            pltpu.VMEM((G, 1, tq), F32),
            pltpu.VMEM((G, vrows, tq), F32),
        ],
        compiler_params=_cparams(("arbitrary", "arbitrary", "arbitrary")),
        name="attn",
    )(qt, kh, vt)
    return out_t.transpose(0, 3, 1, 2).reshape(B, S, D_ATT)


def _first_argmax(x, iota, size):
    m = jnp.max(x, axis=0, keepdims=True)
    return m, jnp.min(jnp.where(x == m, iota, size), axis=0, keepdims=True)


def _route_choices(hb, rwt, rb):
    tm = hb.shape[0]
    logits = lax.dot_general(rwt, hb, (((1,), (1,)), ((), ())), preferred_element_type=F32)
    scores = jax.nn.sigmoid(logits)
    biased = scores + rb
    iota_e = lax.broadcasted_iota(jnp.int32, (EPG, tm), 0).astype(F32)
    group_scores = []
    for gi in range(N_GROUPS):
        bg = biased[gi * EPG:(gi + 1) * EPG, :]
        m1, i1 = _first_argmax(bg, iota_e, EPG)
        m2 = jnp.max(jnp.where(iota_e == i1, -jnp.inf, bg), axis=0, keepdims=True)
        group_scores.append(m1 + m2)
    gs = jnp.concatenate(group_scores, axis=0)
    iota_g = lax.broadcasted_iota(jnp.int32, (N_GROUPS, tm), 0).astype(F32)
    _, g_sel = _first_argmax(gs, iota_g, N_GROUPS)
    in_b = jnp.zeros((EPG, tm), F32)
    in_s = jnp.zeros((EPG, tm), F32)
    for gi in range(N_GROUPS):
        sel = g_sel == gi
        in_b = jnp.where(sel, biased[gi * EPG:(gi + 1) * EPG, :], in_b)
        in_s = jnp.where(sel, scores[gi * EPG:(gi + 1) * EPG, :], in_s)
    _, l1 = _first_argmax(in_b, iota_e, EPG)
    _, l2 = _first_argmax(jnp.where(iota_e == l1, -jnp.inf, in_b), iota_e, EPG)
    w1 = jnp.sum(jnp.where(iota_e == l1, in_s, 0.0), axis=0, keepdims=True)
    w2 = jnp.sum(jnp.where(iota_e == l2, in_s, 0.0), axis=0, keepdims=True)
    wsum = w1 + w2
    experts = jnp.concatenate([g_sel * EPG + l1, g_sel * EPG + l2], axis=0)
    return experts, jnp.concatenate([w1 / wsum, w2 / wsum], axis=0)


def _outproj_route_kernel(*refs, n_in):
    y_refs = refs[:n_in]
    (w_ref, x_ref, gate_ref, g2_ref, sc2_ref, sh2_ref, rwt_ref, rb_ref, tri_ref,
     o_ref, idx_ref, wt_ref, rank_ref, cnt_ref, run_ref) = refs[n_in:]
    acc = None
    off = 0
    for yr in y_refs:
        kd = yr.shape[-1]
        part = _dot(yr[0], w_ref[off:off + kd, :])
        acc = part if acc is None else acc + part
        off += kd
    x_new = x_ref[0] + gate_ref[0] * acc
    o_ref[0] = x_new

    hb = _norm_mod(x_new, g2_ref[...], sc2_ref[0], sh2_ref[0]).astype(BF16)
    tm = hb.shape[0]
    experts, weights = _route_choices(hb, rwt_ref[...], rb_ref[...])
    idx_ref[...] = experts.astype(jnp.int32)
    wt_ref[...] = weights

    @pl.when((pl.program_id(0) == 0) & (pl.program_id(1) == 0))
    def _():
        run_ref[...] = jnp.zeros_like(run_ref)

    iota = lax.broadcasted_iota(jnp.int32, (N_EXPERTS, tm), 0).astype(F32)
    oh0 = jnp.where(iota == experts[0:1, :], 1.0, 0.0)
    oh1 = jnp.where(iota == experts[1:2, :], 1.0, 0.0)
    tri = tri_ref[...]
    pre0 = _dot(oh0.astype(BF16), tri)
    pre1 = _dot(oh1.astype(BF16), tri)
    tot0 = jnp.sum(oh0, axis=1, keepdims=True)
    tot1 = jnp.sum(oh1, axis=1, keepdims=True)
    run = run_ref[...]
    rank0 = jnp.sum(oh0 * (run + pre0), axis=0, keepdims=True)
    rank1 = jnp.sum(oh1 * (run + tot0 + pre1), axis=0, keepdims=True)
    rank_ref[...] = jnp.concatenate([rank0, rank1], axis=0).astype(jnp.int32)
    run = run + tot0 + tot1
    run_ref[...] = run
    cnt_ref[...] = run


def _outproj_route(ys, w, x, gate, g2, scale2, shift2, rwt, rb):
    B, S, D = x.shape
    tm = ROW_TILE
    n = S // tm
    row = lambda b, i: (b, i, 0)
    per_b = lambda b, i: (b, 0, 0)
    const2 = lambda b, i: (0, 0)
    tok = lambda b, i: (0, b * n + i)
    tri = (jnp.arange(tm)[:, None] < jnp.arange(tm)[None, :]).astype(BF16)
    in_specs = [pl.BlockSpec((1, tm, y.shape[-1]), row) for y in ys]
    in_specs += [
        pl.BlockSpec(w.shape, const2),
        pl.BlockSpec((1, tm, D), row),
        pl.BlockSpec((1, 1, D), per_b),
        pl.BlockSpec((1, D), const2),
        pl.BlockSpec((1, 1, D), per_b),
        pl.BlockSpec((1, 1, D), per_b),
        pl.BlockSpec((N_EXPERTS, D), const2),
        pl.BlockSpec((N_EXPERTS, 1), const2),
        pl.BlockSpec((tm, tm), const2),
    ]
    return pl.pallas_call(
        functools.partial(_outproj_route_kernel, n_in=len(ys)),
        grid=(B, n),
        in_specs=in_specs,
        out_specs=[
            pl.BlockSpec((1, tm, D), row),
            pl.BlockSpec((TOP_K, tm), tok),
            pl.BlockSpec((TOP_K, tm), tok),
            pl.BlockSpec((TOP_K, tm), tok),
            pl.BlockSpec((N_EXPERTS, 1), const2),
        ],
        out_shape=[
            jax.ShapeDtypeStruct((B, S, D), F32),
            jax.ShapeDtypeStruct((TOP_K, B * S), jnp.int32),
            jax.ShapeDtypeStruct((TOP_K, B * S), F32),
            jax.ShapeDtypeStruct((TOP_K, B * S), jnp.int32),
            jax.ShapeDtypeStruct((N_EXPERTS, 1), F32),
        ],
        scratch_shapes=[pltpu.VMEM((N_EXPERTS, 1), F32)],
        compiler_params=_cparams(("arbitrary", "arbitrary")),
        name="outproj_route",
    )(*ys, w, x, gate, g2, scale2, shift2, rwt, rb, tri)


def _inproj_ret_kernel(x_ref, g_ref, sc_ref, sh_ref, w_ref, cos_ref, sin_ref,
                       q_ref, k_ref, v_ref, gate_ref):
    h = _norm_mod(x_ref[0], g_ref[...], sc_ref[0], sh_ref[0]).astype(BF16)
    cos = cos_ref[...]
    sin = sin_ref[...]
    half = RET_QK // 2
    dqk = RET_HEADS * RET_QK
    dv = RET_HEADS * RET_V

    def rope(t):
        t1 = t[:, :half]
        t2 = t[:, half:]
        return jnp.concatenate([t1 * cos - t2 * sin, t2 * cos + t1 * sin], axis=-1)

    for hd in range(RET_HEADS):
        c = hd * RET_QK
        q_ref[0, :, c:c + RET_QK] = rope(_dot(h, w_ref[:, c:c + RET_QK])).astype(BF16)
        kh = rope(_dot(h, w_ref[:, dqk + c:dqk + c + RET_QK])) * (RET_QK ** -0.5)
        k_ref[0, :, c:c + RET_QK] = kh.astype(BF16)
    for hd in range(RET_HEADS):
        c = hd * RET_V
        v_ref[0, :, c:c + RET_V] = _dot(h, w_ref[:, 2 * dqk + c:2 * dqk + c + RET_V]).astype(BF16)
        gate_ref[0, :, c:c + RET_V] = _dot(h, w_ref[:, 2 * dqk + dv + c:2 * dqk + dv + c + RET_V])


def _inproj_ret(x, g, scale, shift, w, cos_t, sin_t):
    B, S, D = x.shape
    tm = ROW_TILE
    N = w.shape[1]
    dqk = RET_HEADS * RET_QK
    dv = RET_HEADS * RET_V
    half = RET_QK // 2
    row = lambda b, i: (b, i, 0)
    per_b = lambda b, i: (b, 0, 0)
    const2 = lambda b, i: (0, 0)
    return pl.pallas_call(
        _inproj_ret_kernel,
        grid=(B, S // tm),
        in_specs=[
            pl.BlockSpec((1, tm, D), row),
            pl.BlockSpec((1, D), const2),
            pl.BlockSpec((1, 1, D), per_b),
            pl.BlockSpec((1, 1, D), per_b),
            pl.BlockSpec((D, N), const2),
            pl.BlockSpec((tm, half), lambda b, i: (i, 0)),
            pl.BlockSpec((tm, half), lambda b, i: (i, 0)),
        ],
        out_specs=[
            pl.BlockSpec((1, tm, dqk), row),
            pl.BlockSpec((1, tm, dqk), row),
            pl.BlockSpec((1, tm, dv), row),
            pl.BlockSpec((1, tm, dv), row),
        ],
        out_shape=[
            jax.ShapeDtypeStruct((B, S, dqk), BF16),
            jax.ShapeDtypeStruct((B, S, dqk), BF16),
            jax.ShapeDtypeStruct((B, S, dv), BF16),
            jax.ShapeDtypeStruct((B, S, dv), F32),
        ],
        compiler_params=_cparams(("arbitrary", "arbitrary")),
        name="inproj_ret",
    )(x, g, scale, shift, w, cos_t, sin_t)


def _ret_kernel(*refs, reverse):
    if reverse:
        q_ref, kt_ref, v_ref, dm_ref, qd_ref, kd_ref, cd_ref, yf_ref, g_ref, o_ref, st_ref = refs
    else:
        q_ref, kt_ref, v_ref, dm_ref, qd_ref, kd_ref, cd_ref, o_ref, st_ref = refs
    C = RET_CHUNK
    n_sub = q_ref.shape[1] // C

    @pl.when(pl.program_id(2) == 0)
    def _():
        st_ref[...] = jnp.zeros_like(st_ref)

    dm = dm_ref[0]
    qd = qd_ref[0]
    kd = kd_ref[0]
    cd = cd_ref[0]
    order = range(n_sub - 1, -1, -1) if reverse else range(n_sub)
    for ci in order:
        r = slice(ci * C, (ci + 1) * C)
        qc = q_ref[0, r, :]
        ktc = kt_ref[0, 0, :, r]
        vc = v_ref[0, r, :]
        st = st_ref[...]
        scores = _dot(qc, ktc) * dm
        intra = _dot(scores.astype(BF16), vc)
        kv = _dot((ktc.astype(F32) * kd).astype(BF16), vc)
        out = intra + _dot(qc, st.astype(BF16)) * qd
        st_ref[...] = st * cd + kv
        if reverse:
            y = yf_ref[0, r, :] + out
            y = y * lax.rsqrt(jnp.mean(y * y, axis=-1, keepdims=True) + NORM_EPS)
            o_ref[0, r, :] = (y * _silu(g_ref[0, r, :])).astype(o_ref.dtype)
        else:
            o_ref[0, r, :] = out


def _retention_pass(q, kt, v, tabs, reverse, yf=None, gate=None):
    B, S, _ = q.shape
    ts = min(RET_STEP_ROWS, S)
    n = S // ts
    C = RET_CHUNK
    step = (lambda i: n - 1 - i) if reverse else (lambda i: i)
    rows = lambda width: pl.BlockSpec((1, ts, width), lambda b, h, i: (b, step(i), h))
    per_h = lambda shape: pl.BlockSpec((1,) + shape, lambda b, h, i: (h, 0, 0))
    in_specs = [
        rows(RET_QK),
        pl.BlockSpec((1, 1, RET_QK, ts), lambda b, h, i: (b, h, 0, step(i))),
        rows(RET_V),
        per_h((C, C)), per_h((C, 1)), per_h((1, C)), per_h((1, 1)),
    ]
    args = [q, kt, v, *tabs]
    if reverse:
        in_specs += [rows(RET_V), rows(RET_V)]
        args += [yf, gate]
    return pl.pallas_call(
        functools.partial(_ret_kernel, reverse=reverse),
        grid=(B, RET_HEADS, n),
        in_specs=in_specs,
        out_specs=rows(RET_V),
        out_shape=jax.ShapeDtypeStruct((B, S, RET_HEADS * RET_V), BF16 if reverse else F32),
        scratch_shapes=[pltpu.VMEM((RET_QK, RET_V), F32)],
        compiler_params=_cparams(("arbitrary", "arbitrary", "arbitrary")),
        name="retention_bwd" if reverse else "retention_fwd",
    )(*args)


def _retention_tables(offset, strict):
    C = RET_CHUNK
    log_gamma = jnp.log1p(-jnp.exp2(-5.0 - offset - jnp.arange(RET_HEADS, dtype=F32)))
    idx = jnp.arange(C, dtype=F32)
    diff = idx[:, None] - idx[None, :]
    mask = (diff > 0) if strict else (diff >= 0)
    d_intra = jnp.where(mask, jnp.exp(jnp.where(mask, diff, 0.0)[None] * log_gamma[:, None, None]), 0.0)
    q_dec = jnp.exp((idx + 1.0)[None, :] * log_gamma[:, None])
    k_dec = jnp.exp((C - 1.0 - idx)[None, :] * log_gamma[:, None])
    chunk_dec = jnp.exp(C * log_gamma)
    return d_intra, q_dec, k_dec, chunk_dec


def _shape_tables(d_intra, q_dec, k_dec, chunk_dec, flip):
    if flip:
        d_intra = d_intra[:, ::-1, ::-1]
        q_dec = q_dec[:, ::-1]
        k_dec = k_dec[:, ::-1]
    return (d_intra, q_dec[:, :, None], k_dec[:, None, :], chunk_dec[:, None, None])


def _dispatch_plan(idx, rank, counts):
    T = idx.shape[1]
    A = T * TOP_K
    bm = MOE_BM
    nb = (A + N_EXPERTS * bm) // bm
    counts = counts.reshape(N_EXPERTS).astype(jnp.int32)
    padded = ((counts + bm - 1) // bm) * bm
    pad_end = jnp.cumsum(padded)
    pad_start = pad_end - padded
    experts = jnp.arange(N_EXPERTS, dtype=jnp.int32)
    dest = jnp.sum(jnp.where(idx[..., None] == experts, pad_start, 0), axis=-1) + rank
    block_row0 = jnp.arange(nb, dtype=jnp.int32) * bm
    block_expert = jnp.minimum(
        jnp.sum((pad_end[None, :] <= block_row0[:, None]).astype(jnp.int32), axis=1), N_EXPERTS - 1)
    n_used = (pad_end[-1] // bm).astype(jnp.int32).reshape(1)
    tail_block = jnp.maximum(pad_end // bm - 1, 0).astype(jnp.int32)
    tm = COMB_TM
    dest_tiles = dest.reshape(TOP_K, T // tm, tm).transpose(1, 0, 2).reshape(T // tm, 1, TOP_K * tm)
    return block_expert, n_used, tail_block, dest_tiles


def _zero_block_kernel(tail_ref, o_ref):
    del tail_ref
    o_ref[...] = jnp.zeros_like(o_ref)


def _zero_tail_blocks(tail_block, n_rows, width):
    grid_spec = pltpu.PrefetchScalarGridSpec(
        num_scalar_prefetch=1,
        grid=(N_EXPERTS,),
        in_specs=[],
        out_specs=pl.BlockSpec((MOE_BM, width), lambda e, tail: (tail[e], 0)),
    )
    return pl.pallas_call(
        _zero_block_kernel,
        grid_spec=grid_spec,
        out_shape=jax.ShapeDtypeStruct((n_rows, width), F32),
        compiler_params=_cparams(("arbitrary",)),
        name="zero_tail",
    )(tail_block)


def _row_copy(src, src_row, dst, dst_row, sem):
    return pltpu.make_async_copy(src.at[pl.ds(src_row, 1), :], dst.at[pl.ds(dst_row, 1), :], sem)


def _dispatch_kernel(dest_ref, x_ref, g_ref, sc_ref, sh_ref, gate_ref, sg_ref, su_ref, sd_ref, xd_in_ref,
                     xb_ref, xd_ref, hbuf, sem, *, n_steps):
    del xd_in_ref
    step = pl.program_id(0) * pl.num_programs(1) + pl.program_id(1)
    slot = step % 2
    x = x_ref[0]
    tm = x.shape[0]

    def drain(s):
        for _ in range(TOP_K * tm):
            _row_copy(hbuf.at[s], 0, xd_ref, 0, sem.at[s]).wait()

    @pl.when(step >= 2)
    def _():
        drain(slot)

    h = _norm_mod(x, g_ref[...], sc_ref[0], sh_ref[0])
    hbuf[slot] = h
    for r in range(tm):
        for k in range(TOP_K):
            _row_copy(hbuf.at[slot], r, xd_ref, dest_ref[0, 0, k * tm + r], sem.at[slot]).start()
    hb = h.astype(BF16)
    act = _silu(_dot(hb, sg_ref[...])) * _dot(hb, su_ref[...])
    xb_ref[0] = x + gate_ref[0] * _dot(act.astype(BF16), sd_ref[...])

    @pl.when(step == n_steps - 1)
    def _():
        drain(slot)
        if n_steps > 1:
            drain(1 - slot)


def _dispatch_shared(dest_tiles, x, g, scale, shift, gate, sg, su, sd, x_disp):
    B, S, D = x.shape
    tm = COMB_TM
    n = S // tm
    row = lambda b, i: (b, i, 0)
    per_b = lambda b, i: (b, 0, 0)
    const2 = lambda b, i: (0, 0)
    return pl.pallas_call(
        functools.partial(_dispatch_kernel, n_steps=B * n),
        grid=(B, n),
        in_specs=[
            pl.BlockSpec((1, 1, TOP_K * tm), lambda b, i: (b * n + i, 0, 0), memory_space=pltpu.SMEM),
            pl.BlockSpec((1, tm, D), row),
            pl.BlockSpec((1, D), const2),
            pl.BlockSpec((1, 1, D), per_b),
            pl.BlockSpec((1, 1, D), per_b),
            pl.BlockSpec((1, 1, D), per_b),
            pl.BlockSpec((D, D_FF), const2),
            pl.BlockSpec((D, D_FF), const2),
            pl.BlockSpec((D_FF, D), const2),
            pl.BlockSpec(memory_space=pl.ANY),
        ],
        out_specs=[pl.BlockSpec((1, tm, D), row), pl.BlockSpec(memory_space=pl.ANY)],
        out_shape=[jax.ShapeDtypeStruct((B, S, D), F32), jax.ShapeDtypeStruct(x_disp.shape, F32)],
        input_output_aliases={9: 1},
        scratch_shapes=[pltpu.VMEM((2, tm, D), F32), pltpu.SemaphoreType.DMA((2,))],
        compiler_params=_cparams(("arbitrary", "arbitrary")),
        name="moe_dispatch",
    )(dest_tiles, x, g, scale, shift, gate, sg, su, sd, x_disp)


def _expert_kernel(be_ref, nu_ref, x_ref, wg_ref, wu_ref, wd_ref, o_ref):
    del be_ref

    @pl.when(pl.program_id(0) < nu_ref[0])
    def _():
        xb = x_ref[...].astype(BF16)
        act = _silu(_dot(xb, wg_ref[0])) * _dot(xb, wu_ref[0])
        o_ref[...] = _dot(act.astype(BF16), wd_ref[0])

    @pl.when(pl.program_id(0) >= nu_ref[0])
    def _():
        o_ref[...] = jnp.zeros_like(o_ref)


def _experts(block_expert, n_used, x_disp, wg, wu, wd):
    P, D = x_disp.shape
    bm = MOE_BM
    nb = P // bm
    blk = lambda i, be, nu: (jnp.minimum(i, jnp.maximum(nu[0] - 1, 0)), 0)
    wsel = lambda i, be, nu: (be[jnp.minimum(i, jnp.maximum(nu[0] - 1, 0))], 0, 0)
    grid_spec = pltpu.PrefetchScalarGridSpec(
        num_scalar_prefetch=2,
        grid=(nb,),
        in_specs=[
            pl.BlockSpec((bm, D), blk),
            pl.BlockSpec((1, D, D_FF), wsel),
            pl.BlockSpec((1, D, D_FF), wsel),
            pl.BlockSpec((1, D_FF, D), wsel),
        ],
        out_specs=pl.BlockSpec((bm, D), lambda i, be, nu: (i, 0)),
    )
    return pl.pallas_call(
        _expert_kernel,
        grid_spec=grid_spec,
        out_shape=jax.ShapeDtypeStruct((P, D), F32),
        compiler_params=_cparams(("arbitrary",)),
        name="experts",
    )(block_expert, n_used, x_disp, wg, wu, wd)


def _combine_kernel(pos_cur_ref, pos_next_ref, y_hbm, w_ref, xb_ref, gate_ref, o_ref, ybuf, sem):
    i = pl.program_id(0)
    n = pl.num_programs(0)
    slot = i % 2
    rows = ybuf.shape[1]
    tm = rows // TOP_K

    def issue(pos_ref, dst_slot):
        for r in range(rows):
            _row_copy(y_hbm, pos_ref[0, 0, r], ybuf.at[dst_slot], r, sem.at[dst_slot]).start()

    @pl.when(i == 0)
    def _():
        issue(pos_cur_ref, 0)

    @pl.when(i + 1 < n)
    def _():
        issue(pos_next_ref, 1 - slot)

    for _ in range(rows):
        _row_copy(y_hbm, 0, ybuf.at[slot], 0, sem.at[slot]).wait()
    w = w_ref[...]
    y = ybuf[slot, 0:tm, :] * w[:, 0:1] + ybuf[slot, tm:rows, :] * w[:, 1:2]
    o_ref[...] = xb_ref[...] + gate_ref[0] * y


def _combine(pos_tiles, y_disp, w_cols, xb, gate, tiles_per_seq):
    T, D = xb.shape
    tm = COMB_TM
    n = T // tm
    smem_blk = lambda f: pl.BlockSpec((1, 1, TOP_K * tm), f, memory_space=pltpu.SMEM)
    return pl.pallas_call(
        _combine_kernel,
        grid=(n,),
        in_specs=[
            smem_blk(lambda i: (i, 0, 0)),
            smem_blk(lambda i: (jnp.minimum(i + 1, n - 1), 0, 0)),
            pl.BlockSpec(memory_space=pl.ANY),
            pl.BlockSpec((tm, TOP_K), lambda i: (i, 0)),
            pl.BlockSpec((tm, D), lambda i: (i, 0)),
            pl.BlockSpec((1, 1, D), lambda i: (i // tiles_per_seq, 0, 0)),
        ],
        out_specs=pl.BlockSpec((tm, D), lambda i: (i, 0)),
        out_shape=jax.ShapeDtypeStruct((T, D), F32),
        scratch_shapes=[pltpu.VMEM((2, TOP_K * tm, D), F32), pltpu.SemaphoreType.DMA((2,))],
        compiler_params=_cparams(("arbitrary",)),
        name="combine",
    )(pos_tiles, pos_tiles, y_disp, w_cols, xb, gate)


def _axial_angles(S, hd):
    rows = S // GRID_W
    row = jnp.repeat(jnp.arange(rows), GRID_W).astype(F32)
    col = jnp.tile(jnp.arange(GRID_W), rows).astype(F32)
    n = hd // 4
    inv = ROPE_THETA ** (-jnp.arange(n, dtype=F32) / n)
    ang = jnp.concatenate([row[:, None] * inv, col[:, None] * inv], axis=-1)
    return jnp.cos(ang), jnp.sin(ang)


def _block_diag_gates(gate_w):
    per = LANES // RG_BW
    n_lg = D_RNN // LANES
    g = gate_w.reshape(2, 2, n_lg, per, RG_BW, RG_BW)
    eye = jnp.eye(per, dtype=gate_w.dtype)
    bd = jnp.einsum('dglpij,pq->dglpiqj', g, eye)
    return bd.reshape(2, 2, n_lg, LANES, LANES).astype(BF16)


def kernel(x_prompt, x_sample, c_prompt, c_sample, norm_mix, norm_ffn, w_ada, b_ada, w_in_ab, conv_w,
           conv_b, rg_gate_w, rg_gate_b, rg_lambda, q_norm, k_norm, w_out_ab, w_in_ret, w_out_ret,
           router_w, router_bias, exp_w_gate, exp_w_up, exp_w_down, sh_w_gate, sh_w_up, sh_w_down):
    assert x_prompt.shape[1:] == x_sample.shape[1:]
    Bp_, S, D = x_prompt.shape
    B = Bp_ + x_sample.shape[0]
    T = B * S
    x = jnp.concatenate([x_prompt, x_sample], axis=0)
    c = jnp.concatenate([c_prompt, c_sample], axis=0)
    c_rows = -(-B // 16) * 16
    c_pad = jnp.zeros((c_rows, D), F32).at[:B].set(c)
    mod = _ada_mod(c_pad, w_ada, b_ada)[:, :B].reshape(DEPTH, B, 6, 1, D)

    cos_a, sin_a = _axial_angles(S, HEAD_DIM)
    cos_att = jnp.tile(jnp.concatenate([cos_a, cos_a], axis=-1), (1, ATT_HEADS))
    sin_att = jnp.tile(jnp.concatenate([-sin_a, sin_a], axis=-1), (1, ATT_HEADS))
    cos_r, sin_r = _axial_angles(S, RET_QK)
    seg = jnp.arange(D_ATT) // HEAD_DIM
    ones_bd = (seg[:, None] == seg[None, :]).astype(BF16)
    tabs_f = _shape_tables(*_retention_tables(0.0, False), flip=False)
    tabs_b = _shape_tables(*_retention_tables(0.5, True), flip=True)
    rwt = router_w.T.astype(BF16)
    rb = router_bias.reshape(N_EXPERTS, 1).astype(F32)

    for l in range(DEPTH):
        shift1, scale1, gate1, shift2, scale2, gate2 = (mod[l, :, j] for j in range(6))
        g_mix = norm_mix[l].reshape(1, D)
        if l % 2 == 0:
            e = l // 2
            rg, q, k, v = _inproj_even(
                x, g_mix, scale1, shift1, w_in_ab[e].astype(BF16), ones_bd,
                jnp.tile(q_norm[e], ATT_HEADS).reshape(1, D_ATT),
                jnp.tile(k_norm[e], KV_HEADS).reshape(1, KV_HEADS * HEAD_DIM), cos_att, sin_att)
            y_rg = _rglru(rg, conv_w[e], conv_b[e].reshape(1, D_RNN), _block_diag_gates(rg_gate_w[e]),
                          rg_gate_b[e].reshape(2, 2, 1, D_RNN), rg_lambda[e].reshape(2, 1, D_RNN))
            y_att = _attention(q, k, v)
            ys, w_out = [y_rg, y_att], w_out_ab[e]
        else:
            o = l // 2
            q, k, v, gate = _inproj_ret(x, g_mix, scale1, shift1, w_in_ret[o].astype(BF16), cos_r, sin_r)
            kt = k.reshape(B, S, RET_HEADS, RET_QK).transpose(0, 2, 3, 1)
            y_f = _retention_pass(q, kt, v, tabs_f, reverse=False)
            y_n = _retention_pass(q, kt, v, tabs_b, reverse=True, yf=y_f, gate=gate)
            ys, w_out = [y_n], w_out_ret[o]

        g_ffn = norm_ffn[l].reshape(1, D)
        x, idx, wts, rank, counts = _outproj_route(
            ys, w_out.astype(BF16), x, gate1, g_ffn, scale2, shift2, rwt, rb)
        block_expert, n_used, tail_block, dest_tiles = _dispatch_plan(idx, rank, counts)
        x_disp = _zero_tail_blocks(tail_block, T * TOP_K + N_EXPERTS * MOE_BM, D)
        xb, x_disp = _dispatch_shared(
            dest_tiles, x, g_ffn, scale2, shift2, gate2,
            sh_w_gate[l].astype(BF16), sh_w_up[l].astype(BF16), sh_w_down[l].astype(BF16), x_disp)
        y_disp = _experts(block_expert, n_used, x_disp,
                          exp_w_gate[l].astype(BF16), exp_w_up[l].astype(BF16), exp_w_down[l].astype(BF16))
        x = _combine(dest_tiles, y_disp, wts.T, xb.reshape(T, D), gate2, S // COMB_TM).reshape(B, S, D)

    return x[:Bp_], x[Bp_:]
```

```python
import functools

import jax
import jax.numpy as jnp
from jax import lax
from jax.experimental import pallas as pl
from jax.experimental.pallas import tpu as pltpu

F32 = jnp.float32
BF16 = jnp.bfloat16

D_MODEL = 1024
DEPTH = 4
GRID_W = 64
NORM_EPS = 1e-6
ROPE_THETA = 10000.0
D_RNN = 512
RG_BW = 64
CONV_W = 4
RG_C = 8.0
ATT_HEADS = 8
KV_HEADS = 2
HEAD_DIM = 64
D_ATT = ATT_HEADS * HEAD_DIM
RET_HEADS = 4
RET_QK = 256
RET_V = 512
RET_CHUNK = 128
N_EXPERTS = 32
N_GROUPS = 4
EPG = N_EXPERTS // N_GROUPS
TOP_K = 2
D_FF = 512

LANES = 128
SUBLANES = 8
VMEM_LIMIT = 52 * 1024 * 1024

ROW_TILE = 256
ATT_TQ = 256
ATT_TK = 256
ATT_LOOKAHEAD = 6
ATT_ONES_ROWS = 16
SCAN_ROWS = 128
SCAN_UNROLL = 4
RET_STEP_ROWS = 2048
MOE_BM = 512
COMB_TM = 256
NEG_BIG = -1e30
ATT_Q_SCALE = HEAD_DIM ** -0.5 * 1.4426950408889634


def _cparams(sem):
    return pltpu.CompilerParams(dimension_semantics=sem, vmem_limit_bytes=VMEM_LIMIT)


def _dot(a, b):
    return jnp.dot(a, b, preferred_element_type=F32)


def _silu(x):
    return x * jax.nn.sigmoid(x)


def _norm_mod(x, g, scale, shift):
    ms = jnp.mean(x * x, axis=-1, keepdims=True)
    h = x * lax.rsqrt(ms + NORM_EPS) * g
    return h * (1.0 + scale) + shift


def _ada_kernel(c_ref, w_ref, b_ref, o_ref):
    c = c_ref[...]
    o_ref[0] = _dot(_silu(c).astype(BF16), w_ref[0].astype(BF16)) + b_ref[0]


def _ada_mod(c_pad, w_ada, b_ada):
    L, D, N = w_ada.shape
    Bp = c_pad.shape[0]
    tn = D_MODEL
    return pl.pallas_call(
        _ada_kernel,
        grid=(L, N // tn),
        in_specs=[
            pl.BlockSpec((Bp, D), lambda l, j: (0, 0)),
            pl.BlockSpec((1, D, tn), lambda l, j: (l, 0, j)),
            pl.BlockSpec((1, 1, tn), lambda l, j: (l, 0, j)),
        ],
        out_specs=pl.BlockSpec((1, Bp, tn), lambda l, j: (l, 0, j)),
        out_shape=jax.ShapeDtypeStruct((L, Bp, N), F32),
        compiler_params=_cparams(("arbitrary", "arbitrary")),
        name="ada",
    )(c_pad, w_ada, b_ada.reshape(L, 1, N))


def _segment_mean_sq(x, ones_bd, width):
    sq = x * x
    hi = sq.astype(BF16)
    lo = (sq - hi.astype(F32)).astype(BF16)
    return (_dot(hi, ones_bd) + _dot(lo, ones_bd)) * (1.0 / width)


def _rope_halves(x, cos2, sin_signed, half):
    w = x.shape[-1]
    lane = lax.broadcasted_iota(jnp.int32, x.shape, 1)
    first = (lane % (2 * half)) < half
    rot = jnp.where(first, pltpu.roll(x, w - half, 1), pltpu.roll(x, half, 1))
    return x * cos2 + rot * sin_signed


def _inproj_even_kernel(x_ref, g_ref, sc_ref, sh_ref, w_ref, ones_ref, qn_ref, kn_ref,
                        cos_ref, sin_ref, rg_ref, q_ref, k_ref, v_ref):
    h = _norm_mod(x_ref[0], g_ref[...], sc_ref[0], sh_ref[0]).astype(BF16)
    rg_ref[0] = _dot(h, w_ref[:, 0:2 * D_RNN])
    c0 = 2 * D_RNN
    q = _dot(h, w_ref[:, c0:c0 + D_ATT])
    c1 = c0 + D_ATT
    kw = KV_HEADS * HEAD_DIM
    k = _dot(h, w_ref[:, c1:c1 + kw])
    v_ref[0] = _dot(h, w_ref[:, c1 + kw:c1 + 2 * kw]).astype(BF16)

    half = HEAD_DIM // 2
    ones = ones_ref[...]
    qn = q * lax.rsqrt(_segment_mean_sq(q, ones, HEAD_DIM) + NORM_EPS) * qn_ref[...]
    qr = _rope_halves(qn, cos_ref[...], sin_ref[...], half)
    q_ref[0] = (qr * ATT_Q_SCALE).astype(BF16)
    kn = k * lax.rsqrt(_segment_mean_sq(k, ones[:kw, :kw], HEAD_DIM) + NORM_EPS) * kn_ref[...]
    kr = _rope_halves(kn, cos_ref[:, :kw], sin_ref[:, :kw], half)
    k_ref[0] = kr.astype(BF16)


def _inproj_even(x, g, scale, shift, w, ones_bd, qn, kn, cos_t, sin_t):
    B, S, D = x.shape
    tm = ROW_TILE
    N = w.shape[1]
    kw = KV_HEADS * HEAD_DIM
    row = lambda b, i: (b, i, 0)
    per_b = lambda b, i: (b, 0, 0)
    const2 = lambda b, i: (0, 0)
    return pl.pallas_call(
        _inproj_even_kernel,
        grid=(B, S // tm),
        in_specs=[
            pl.BlockSpec((1, tm, D), row),
            pl.BlockSpec((1, D), const2),
            pl.BlockSpec((1, 1, D), per_b),
            pl.BlockSpec((1, 1, D), per_b),
            pl.BlockSpec((D, N), const2),
            pl.BlockSpec((D_ATT, D_ATT), const2),
            pl.BlockSpec((1, D_ATT), const2),
            pl.BlockSpec((1, kw), const2),
            pl.BlockSpec((tm, D_ATT), lambda b, i: (i, 0)),
            pl.BlockSpec((tm, D_ATT), lambda b, i: (i, 0)),
        ],
        out_specs=[
            pl.BlockSpec((1, tm, 2 * D_RNN), row),
            pl.BlockSpec((1, tm, D_ATT), row),
            pl.BlockSpec((1, tm, kw), row),
            pl.BlockSpec((1, tm, kw), row),
        ],
        out_shape=[
            jax.ShapeDtypeStruct((B, S, 2 * D_RNN), F32),
            jax.ShapeDtypeStruct((B, S, D_ATT), BF16),
            jax.ShapeDtypeStruct((B, S, kw), BF16),
            jax.ShapeDtypeStruct((B, S, kw), BF16),
        ],
        compiler_params=_cparams(("arbitrary", "arbitrary")),
        name="inproj_even",
    )(x, g, scale, shift, w, ones_bd, qn, kn, cos_t, sin_t)


def _scan_chunk(a, b, carry, reverse):
    R = a.shape[0]
    row = lax.broadcasted_iota(jnp.int32, a.shape, 0) % SUBLANES
    s = 1
    while s < SUBLANES:
        if reverse:
            keep = row < SUBLANES - s
            shift = R - s
        else:
            keep = row >= s
            shift = s
        a_sh = jnp.where(keep, pltpu.roll(a, shift, 0), 1.0)
        b_sh = jnp.where(keep, pltpu.roll(b, shift, 0), 0.0)
        b = a * b_sh + b
        a = a * a_sh
        s *= 2
    n_groups = R // SUBLANES
    hs = [None] * n_groups
    for gi in (range(n_groups - 1, -1, -1) if reverse else range(n_groups)):
        rows = slice(gi * SUBLANES, (gi + 1) * SUBLANES)
        h = a[rows] * carry + b[rows]
        carry = h[0:1, :] if reverse else h[SUBLANES - 1:SUBLANES, :]
        hs[gi] = h
    return jnp.concatenate(hs, axis=0), carry


def _rglru_kernel(x_ref, gate_ref, cw_ref, cb_ref, gw_ref, gb_ref, lam_ref, y_ref, hf_ref, carry_ref):
    S = x_ref.shape[1]
    R = SCAN_ROWS
    n_chunks = S // R
    cw = cw_ref[...]
    cb = cb_ref[...]

    def conv_chunk(c):
        r0 = pl.multiple_of(c * R, R)
        main = x_ref[0, pl.ds(r0, R), :]
        lo = pl.multiple_of(jnp.maximum(r0 - SUBLANES, 0), SUBLANES)
        hi = pl.multiple_of(jnp.minimum(r0 + R, S - SUBLANES), SUBLANES)
        prev = jnp.where(c > 0, x_ref[0, pl.ds(lo, SUBLANES), :], 0.0)
        nxt = jnp.where(c < n_chunks - 1, x_ref[0, pl.ds(hi, SUBLANES), :], 0.0)
        ext = jnp.concatenate([prev, main, nxt], axis=0)
        xc = cb
        for tap in range(CONV_W):
            o = SUBLANES - 2 + tap
            xc = xc + ext[o:o + R, :] * cw[tap:tap + 1, :]
        return r0, xc

    def direction(c, d, reverse):
        r0, xc = conv_chunk(c)
        xb = xc.astype(BF16)
        r = jax.nn.sigmoid(_dot(xb, gw_ref[d, 0, 0]) + gb_ref[d, 0])
        i = jax.nn.sigmoid(_dot(xb, gw_ref[d, 1, 0]) + gb_ref[d, 1])
        log_a = -RG_C * r * jax.nn.softplus(-lam_ref[d])
        a = jnp.exp(log_a)
        b = jnp.sqrt(jnp.tanh(-log_a) * (a * a + 1.0)) * (i * xc)
        h, carry = _scan_chunk(a, b, carry_ref[...], reverse)
        carry_ref[...] = carry
        return r0, h

    carry_ref[...] = jnp.zeros_like(carry_ref)

    def fwd_body(c, _):
        r0, h = direction(c, 0, False)
        hf_ref[pl.ds(r0, R), :] = h
        return 0

    lax.fori_loop(0, n_chunks, fwd_body, 0, unroll=SCAN_UNROLL)
    carry_ref[...] = jnp.zeros_like(carry_ref)

    def bwd_body(j, _):
        c = n_chunks - 1 - j
        r0, h = direction(c, 1, True)
        gate = gate_ref[0, pl.ds(r0, R), :]
        y = jax.nn.gelu(gate) * (hf_ref[pl.ds(r0, R), :] + h)
        y_ref[0, pl.ds(r0, R), :] = y.astype(y_ref.dtype)
        return 0

    lax.fori_loop(0, n_chunks, bwd_body, 0, unroll=SCAN_UNROLL)


def _rglru(rg, conv_w, conv_b, gw_bd, gate_b, lam):
    B, S, _ = rg.shape
    n_lg = D_RNN // LANES
    return pl.pallas_call(
        _rglru_kernel,
        grid=(B, n_lg),
        in_specs=[
            pl.BlockSpec((1, S, LANES), lambda b, j: (b, 0, n_lg + j)),
            pl.BlockSpec((1, S, LANES), lambda b, j: (b, 0, j)),
            pl.BlockSpec((CONV_W, LANES), lambda b, j: (0, j)),
            pl.BlockSpec((1, LANES), lambda b, j: (0, j)),
            pl.BlockSpec((2, 2, 1, LANES, LANES), lambda b, j: (0, 0, j, 0, 0)),
            pl.BlockSpec((2, 2, 1, LANES), lambda b, j: (0, 0, 0, j)),
            pl.BlockSpec((2, 1, LANES), lambda b, j: (0, 0, j)),
        ],
        out_specs=pl.BlockSpec((1, S, LANES), lambda b, j: (b, 0, j)),
        out_shape=jax.ShapeDtypeStruct((B, S, D_RNN), BF16),
        scratch_shapes=[pltpu.VMEM((S, LANES), F32), pltpu.VMEM((1, LANES), F32)],
        compiler_params=_cparams(("arbitrary", "arbitrary")),
        name="rglru",
    )(rg, rg, conv_w, conv_b, gw_bd, gate_b, lam)


def _attn_kernel(qt_ref, k_ref, vt_ref, o_ref, m_ref, acc_ref):
    S = k_ref.shape[2]
    G = ATT_HEADS // KV_HEADS
    m_ref[...] = jnp.full_like(m_ref, NEG_BIG)
    acc_ref[...] = jnp.zeros_like(acc_ref)

    def scores(t):
        j, g = divmod(t, G)
        return _dot(k_ref[0, 0, j * ATT_TK:(j + 1) * ATT_TK, :], qt_ref[0, g])

    n_items = (S // ATT_TK) * G
    pending = [scores(t) for t in range(min(ATT_LOOKAHEAD, n_items))]
    for t in range(n_items):
        j, g = divmod(t, G)
        if t + ATT_LOOKAHEAD < n_items:
            pending.append(scores(t + ATT_LOOKAHEAD))
        s = pending.pop(0)
        m_old = m_ref[g]
        m_new = jnp.maximum(m_old, jnp.max(s, axis=0, keepdims=True))
        p = jnp.exp2(s - m_new)
        pv = _dot(vt_ref[0, 0, :, j * ATT_TK:(j + 1) * ATT_TK], p.astype(BF16))
        acc_ref[g] = jnp.exp2(m_old - m_new) * acc_ref[g] + pv
        m_ref[g] = m_new
    for g in range(G):
        acc = acc_ref[g]
        o_ref[0, g] = (acc[:HEAD_DIM, :] / acc[HEAD_DIM:HEAD_DIM + 1, :]).astype(o_ref.dtype)


def _attention(q, k, v):
    B, S, _ = q.shape
    G = ATT_HEADS // KV_HEADS
    tq = ATT_TQ
    qt = q.reshape(B, S, ATT_HEADS, HEAD_DIM).transpose(0, 2, 3, 1)
    kh = k.reshape(B, S, KV_HEADS, HEAD_DIM).transpose(0, 2, 1, 3)
    vt = v.reshape(B, S, KV_HEADS, HEAD_DIM).transpose(0, 2, 3, 1)
    vt = jnp.concatenate([vt, jnp.ones((B, KV_HEADS, ATT_ONES_ROWS, S), v.dtype)], axis=2)
    vrows = HEAD_DIM + ATT_ONES_ROWS
    out_t = pl.pallas_call(
        _attn_kernel,
        grid=(B, KV_HEADS, S // tq),
        in_specs=[
            pl.BlockSpec((1, G, HEAD_DIM, tq), lambda b, h, i: (b, h, 0, i)),
            pl.BlockSpec((1, 1, S, HEAD_DIM), lambda b, h, i: (b, h, 0, 0)),
            pl.BlockSpec((1, 1, vrows, S), lambda b, h, i: (b, h, 0, 0)),
        ],
        out_specs=pl.BlockSpec((1, G, HEAD_DIM, tq), lambda b, h, i: (b, h, 0, i)),
        out_shape=jax.ShapeDtypeStruct((B, ATT_HEADS, HEAD_DIM, S), BF16),
        scratch_shapes=[
            pltpu.VMEM((G, 1, tq), F32),
            pltpu.VMEM((G, vrows, tq), F32),
        ],
        compiler_params=_cparams(("arbitrary", "arbitrary", "arbitrary")),
        name="attn",
    )(qt, kh, vt)
    return out_t.transpose(0, 3, 1, 2).reshape(B, S, D_ATT)


def _first_argmax(x, iota, size):
    m = jnp.max(x, axis=0, keepdims=True)
    return m, jnp.min(jnp.where(x == m, iota, size), axis=0, keepdims=True)


def _route_choices(hb, rwt, rb):
    tm = hb.shape[0]
    logits = lax.dot_general(rwt, hb, (((1,), (1,)), ((), ())), preferred_element_type=F32)
    scores = jax.nn.sigmoid(logits)
    biased = scores + rb
    iota_e = lax.broadcasted_iota(jnp.int32, (EPG, tm), 0).astype(F32)
    group_scores = []
    for gi in range(N_GROUPS):
        bg = biased[gi * EPG:(gi + 1) * EPG, :]
        m1, i1 = _first_argmax(bg, iota_e, EPG)
        m2 = jnp.max(jnp.where(iota_e == i1, -jnp.inf, bg), axis=0, keepdims=True)
        group_scores.append(m1 + m2)
    gs = jnp.concatenate(group_scores, axis=0)
    iota_g = lax.broadcasted_iota(jnp.int32, (N_GROUPS, tm), 0).astype(F32)
    _, g_sel = _first_argmax(gs, iota_g, N_GROUPS)
    in_b = jnp.zeros((EPG, tm), F32)
    in_s = jnp.zeros((EPG, tm), F32)
    for gi in range(N_GROUPS):
        sel = g_sel == gi
        in_b = jnp.where(sel, biased[gi * EPG:(gi + 1) * EPG, :], in_b)
        in_s = jnp.where(sel, scores[gi * EPG:(gi + 1) * EPG, :], in_s)
    _, l1 = _first_argmax(in_b, iota_e, EPG)
    _, l2 = _first_argmax(jnp.where(iota_e == l1, -jnp.inf, in_b), iota_e, EPG)
    w1 = jnp.sum(jnp.where(iota_e == l1, in_s, 0.0), axis=0, keepdims=True)
    w2 = jnp.sum(jnp.where(iota_e == l2, in_s, 0.0), axis=0, keepdims=True)
    wsum = w1 + w2
    experts = jnp.concatenate([g_sel * EPG + l1, g_sel * EPG + l2], axis=0)
    return experts, jnp.concatenate([w1 / wsum, w2 / wsum], axis=0)


def _outproj_route_kernel(*refs, n_in):
    y_refs = refs[:n_in]
    (w_ref, x_ref, gate_ref, g2_ref, sc2_ref, sh2_ref, rwt_ref, rb_ref, tri_ref,
     o_ref, idx_ref, wt_ref, rank_ref, cnt_ref, run_ref) = refs[n_in:]
    acc = None
    off = 0
    for yr in y_refs:
        kd = yr.shape[-1]
        part = _dot(yr[0], w_ref[off:off + kd, :])
        acc = part if acc is None else acc + part
        off += kd
    x_new = x_ref[0] + gate_ref[0] * acc
    o_ref[0] = x_new

    hb = _norm_mod(x_new, g2_ref[...], sc2_ref[0], sh2_ref[0]).astype(BF16)
    tm = hb.shape[0]
    experts, weights = _route_choices(hb, rwt_ref[...], rb_ref[...])
    idx_ref[...] = experts.astype(jnp.int32)
    wt_ref[...] = weights

    @pl.when((pl.program_id(0) == 0) & (pl.program_id(1) == 0))
    def _():
        run_ref[...] = jnp.zeros_like(run_ref)

    iota = lax.broadcasted_iota(jnp.int32, (N_EXPERTS, tm), 0).astype(F32)
    oh0 = jnp.where(iota == experts[0:1, :], 1.0, 0.0)
    oh1 = jnp.where(iota == experts[1:2, :], 1.0, 0.0)
    tri = tri_ref[...]
    pre0 = _dot(oh0.astype(BF16), tri)
    pre1 = _dot(oh1.astype(BF16), tri)
    tot0 = jnp.sum(oh0, axis=1, keepdims=True)
    tot1 = jnp.sum(oh1, axis=1, keepdims=True)
    run = run_ref[...]
    rank0 = jnp.sum(oh0 * (run + pre0), axis=0, keepdims=True)
    rank1 = jnp.sum(oh1 * (run + tot0 + pre1), axis=0, keepdims=True)
    rank_ref[...] = jnp.concatenate([rank0, rank1], axis=0).astype(jnp.int32)
    run = run + tot0 + tot1
    run_ref[...] = run
    cnt_ref[...] = run


def _outproj_route(ys, w, x, gate, g2, scale2, shift2, rwt, rb):
    B, S, D = x.shape
    tm = ROW_TILE
    n = S // tm
    row = lambda b, i: (b, i, 0)
    per_b = lambda b, i: (b, 0, 0)
    const2 = lambda b, i: (0, 0)
    tok = lambda b, i: (0, b * n + i)
    tri = (jnp.arange(tm)[:, None] < jnp.arange(tm)[None, :]).astype(BF16)
    in_specs = [pl.BlockSpec((1, tm, y.shape[-1]), row) for y in ys]
    in_specs += [
        pl.BlockSpec(w.shape, const2),
        pl.BlockSpec((1, tm, D), row),
        pl.BlockSpec((1, 1, D), per_b),
        pl.BlockSpec((1, D), const2),
        pl.BlockSpec((1, 1, D), per_b),
        pl.BlockSpec((1, 1, D), per_b),
        pl.BlockSpec((N_EXPERTS, D), const2),
        pl.BlockSpec((N_EXPERTS, 1), const2),
        pl.BlockSpec((tm, tm), const2),
    ]
    return pl.pallas_call(
        functools.partial(_outproj_route_kernel, n_in=len(ys)),
        grid=(B, n),
        in_specs=in_specs,
        out_specs=[
            pl.BlockSpec((1, tm, D), row),
            pl.BlockSpec((TOP_K, tm), tok),
            pl.BlockSpec((TOP_K, tm), tok),
            pl.BlockSpec((TOP_K, tm), tok),
            pl.BlockSpec((N_EXPERTS, 1), const2),
        ],
        out_shape=[
            jax.ShapeDtypeStruct((B, S, D), F32),
            jax.ShapeDtypeStruct((TOP_K, B * S), jnp.int32),
            jax.ShapeDtypeStruct((TOP_K, B * S), F32),
            jax.ShapeDtypeStruct((TOP_K, B * S), jnp.int32),
            jax.ShapeDtypeStruct((N_EXPERTS, 1), F32),
        ],
        scratch_shapes=[pltpu.VMEM((N_EXPERTS, 1), F32)],
        compiler_params=_cparams(("arbitrary", "arbitrary")),
        name="outproj_route",
    )(*ys, w, x, gate, g2, scale2, shift2, rwt, rb, tri)


def _inproj_ret_kernel(x_ref, g_ref, sc_ref, sh_ref, w_ref, cos_ref, sin_ref,
                       q_ref, k_ref, v_ref, gate_ref):
    h = _norm_mod(x_ref[0], g_ref[...], sc_ref[0], sh_ref[0]).astype(BF16)
    cos = cos_ref[...]
    sin = sin_ref[...]
    half = RET_QK // 2
    dqk = RET_HEADS * RET_QK
    dv = RET_HEADS * RET_V

    def rope(t):
        t1 = t[:, :half]
        t2 = t[:, half:]
        return jnp.concatenate([t1 * cos - t2 * sin, t2 * cos + t1 * sin], axis=-1)

    for hd in range(RET_HEADS):
        c = hd * RET_QK
        q_ref[0, :, c:c + RET_QK] = rope(_dot(h, w_ref[:, c:c + RET_QK])).astype(BF16)
        kh = rope(_dot(h, w_ref[:, dqk + c:dqk + c + RET_QK])) * (RET_QK ** -0.5)
        k_ref[0, :, c:c + RET_QK] = kh.astype(BF16)
    for hd in range(RET_HEADS):
        c = hd * RET_V
        v_ref[0, :, c:c + RET_V] = _dot(h, w_ref[:, 2 * dqk + c:2 * dqk + c + RET_V]).astype(BF16)
        gate_ref[0, :, c:c + RET_V] = _dot(h, w_ref[:, 2 * dqk + dv + c:2 * dqk + dv + c + RET_V])


def _inproj_ret(x, g, scale, shift, w, cos_t, sin_t):
    B, S, D = x.shape
    tm = ROW_TILE
    N = w.shape[1]
    dqk = RET_HEADS * RET_QK
    dv = RET_HEADS * RET_V
    half = RET_QK // 2
    row = lambda b, i: (b, i, 0)
    per_b = lambda b, i: (b, 0, 0)
    const2 = lambda b, i: (0, 0)
    return pl.pallas_call(
        _inproj_ret_kernel,
        grid=(B, S // tm),
        in_specs=[
            pl.BlockSpec((1, tm, D), row),
            pl.BlockSpec((1, D), const2),
            pl.BlockSpec((1, 1, D), per_b),
            pl.BlockSpec((1, 1, D), per_b),
            pl.BlockSpec((D, N), const2),
            pl.BlockSpec((tm, half), lambda b, i: (i, 0)),
            pl.BlockSpec((tm, half), lambda b, i: (i, 0)),
        ],
        out_specs=[
            pl.BlockSpec((1, tm, dqk), row),
            pl.BlockSpec((1, tm, dqk), row),
            pl.BlockSpec((1, tm, dv), row),
            pl.BlockSpec((1, tm, dv), row),
        ],
        out_shape=[
            jax.ShapeDtypeStruct((B, S, dqk), BF16),
            jax.ShapeDtypeStruct((B, S, dqk), BF16),
            jax.ShapeDtypeStruct((B, S, dv), BF16),
            jax.ShapeDtypeStruct((B, S, dv), F32),
        ],
        compiler_params=_cparams(("arbitrary", "arbitrary")),
        name="inproj_ret",
    )(x, g, scale, shift, w, cos_t, sin_t)


def _ret_kernel(*refs, reverse):
    if reverse:
        q_ref, kt_ref, v_ref, dm_ref, qd_ref, kd_ref, cd_ref, yf_ref, g_ref, o_ref, st_ref = refs
    else:
        q_ref, kt_ref, v_ref, dm_ref, qd_ref, kd_ref, cd_ref, o_ref, st_ref = refs
    C = RET_CHUNK
    n_sub = q_ref.shape[1] // C

    @pl.when(pl.program_id(2) == 0)
    def _():
        st_ref[...] = jnp.zeros_like(st_ref)

    dm = dm_ref[0]
    qd = qd_ref[0]
    kd = kd_ref[0]
    cd = cd_ref[0]
    order = range(n_sub - 1, -1, -1) if reverse else range(n_sub)
    for ci in order:
        r = slice(ci * C, (ci + 1) * C)
        qc = q_ref[0, r, :]
        ktc = kt_ref[0, 0, :, r]
        vc = v_ref[0, r, :]
        st = st_ref[...]
        scores = _dot(qc, ktc) * dm
        intra = _dot(scores.astype(BF16), vc)
        kv = _dot((ktc.astype(F32) * kd).astype(BF16), vc)
        out = intra + _dot(qc, st.astype(BF16)) * qd
        st_ref[...] = st * cd + kv
        if reverse:
            y = yf_ref[0, r, :] + out
            y = y * lax.rsqrt(jnp.mean(y * y, axis=-1, keepdims=True) + NORM_EPS)
            o_ref[0, r, :] = (y * _silu(g_ref[0, r, :])).astype(o_ref.dtype)
        else:
            o_ref[0, r, :] = out


def _retention_pass(q, kt, v, tabs, reverse, yf=None, gate=None):
    B, S, _ = q.shape
    ts = min(RET_STEP_ROWS, S)
    n = S // ts
    C = RET_CHUNK
    step = (lambda i: n - 1 - i) if reverse else (lambda i: i)
    rows = lambda width: pl.BlockSpec((1, ts, width), lambda b, h, i: (b, step(i), h))
    per_h = lambda shape: pl.BlockSpec((1,) + shape, lambda b, h, i: (h, 0, 0))
    in_specs = [
        rows(RET_QK),
        pl.BlockSpec((1, 1, RET_QK, ts), lambda b, h, i: (b, h, 0, step(i))),
        rows(RET_V),
        per_h((C, C)), per_h((C, 1)), per_h((1, C)), per_h((1, 1)),
    ]
    args = [q, kt, v, *tabs]
    if reverse:
        in_specs += [rows(RET_V), rows(RET_V)]
        args += [yf, gate]
    return pl.pallas_call(
        functools.partial(_ret_kernel, reverse=reverse),
        grid=(B, RET_HEADS, n),
        in_specs=in_specs,
        out_specs=rows(RET_V),
        out_shape=jax.ShapeDtypeStruct((B, S, RET_HEADS * RET_V), BF16 if reverse else F32),
        scratch_shapes=[pltpu.VMEM((RET_QK, RET_V), F32)],
        compiler_params=_cparams(("arbitrary", "arbitrary", "arbitrary")),
        name="retention_bwd" if reverse else "retention_fwd",
    )(*args)


def _retention_tables(offset, strict):
    C = RET_CHUNK
    log_gamma = jnp.log1p(-jnp.exp2(-5.0 - offset - jnp.arange(RET_HEADS, dtype=F32)))
    idx = jnp.arange(C, dtype=F32)
    diff = idx[:, None] - idx[None, :]
    mask = (diff > 0) if strict else (diff >= 0)
    d_intra = jnp.where(mask, jnp.exp(jnp.where(mask, diff, 0.0)[None] * log_gamma[:, None, None]), 0.0)
    q_dec = jnp.exp((idx + 1.0)[None, :] * log_gamma[:, None])
    k_dec = jnp.exp((C - 1.0 - idx)[None, :] * log_gamma[:, None])
    chunk_dec = jnp.exp(C * log_gamma)
    return d_intra, q_dec, k_dec, chunk_dec


def _shape_tables(d_intra, q_dec, k_dec, chunk_dec, flip):
    if flip:
        d_intra = d_intra[:, ::-1, ::-1]
        q_dec = q_dec[:, ::-1]
        k_dec = k_dec[:, ::-1]
    return (d_intra, q_dec[:, :, None], k_dec[:, None, :], chunk_dec[:, None, None])


def _dispatch_plan(idx, rank, counts):
    T = idx.shape[1]
    A = T * TOP_K
    bm = MOE_BM
    nb = (A + N_EXPERTS * bm) // bm
    counts = counts.reshape(N_EXPERTS).astype(jnp.int32)
    padded = ((counts + bm - 1) // bm) * bm
    pad_end = jnp.cumsum(padded)
    pad_start = pad_end - padded
    experts = jnp.arange(N_EXPERTS, dtype=jnp.int32)
    dest = jnp.sum(jnp.where(idx[..., None] == experts, pad_start, 0), axis=-1) + rank
    block_row0 = jnp.arange(nb, dtype=jnp.int32) * bm
    block_expert = jnp.minimum(
        jnp.sum((pad_end[None, :] <= block_row0[:, None]).astype(jnp.int32), axis=1), N_EXPERTS - 1)
    n_used = (pad_end[-1] // bm).astype(jnp.int32).reshape(1)
    tail_block = jnp.maximum(pad_end // bm - 1, 0).astype(jnp.int32)
    tm = COMB_TM
    dest_tiles = dest.reshape(TOP_K, T // tm, tm).transpose(1, 0, 2).reshape(T // tm, 1, TOP_K * tm)
    return block_expert, n_used, tail_block, dest_tiles


def _zero_block_kernel(tail_ref, o_ref):
    del tail_ref
    o_ref[...] = jnp.zeros_like(o_ref)


def _zero_tail_blocks(tail_block, n_rows, width):
    grid_spec = pltpu.PrefetchScalarGridSpec(
        num_scalar_prefetch=1,
        grid=(N_EXPERTS,),
        in_specs=[],
        out_specs=pl.BlockSpec((MOE_BM, width), lambda e, tail: (tail[e], 0)),
    )
    return pl.pallas_call(
        _zero_block_kernel,
        grid_spec=grid_spec,
        out_shape=jax.ShapeDtypeStruct((n_rows, width), F32),
        compiler_params=_cparams(("arbitrary",)),
        name="zero_tail",
    )(tail_block)


def _row_copy(src, src_row, dst, dst_row, sem):
    return pltpu.make_async_copy(src.at[pl.ds(src_row, 1), :], dst.at[pl.ds(dst_row, 1), :], sem)


def _dispatch_kernel(dest_ref, x_ref, g_ref, sc_ref, sh_ref, gate_ref, sg_ref, su_ref, sd_ref, xd_in_ref,
                     xb_ref, xd_ref, hbuf, sem, *, n_steps):
    del xd_in_ref
    step = pl.program_id(0) * pl.num_programs(1) + pl.program_id(1)
    slot = step % 2
    x = x_ref[0]
    tm = x.shape[0]

    def drain(s):
        for _ in range(TOP_K * tm):
            _row_copy(hbuf.at[s], 0, xd_ref, 0, sem.at[s]).wait()

    @pl.when(step >= 2)
    def _():
        drain(slot)

    h = _norm_mod(x, g_ref[...], sc_ref[0], sh_ref[0])
    hbuf[slot] = h
    for r in range(tm):
        for k in range(TOP_K):
            _row_copy(hbuf.at[slot], r, xd_ref, dest_ref[0, 0, k * tm + r], sem.at[slot]).start()
    hb = h.astype(BF16)
    act = _silu(_dot(hb, sg_ref[...])) * _dot(hb, su_ref[...])
    xb_ref[0] = x + gate_ref[0] * _dot(act.astype(BF16), sd_ref[...])

    @pl.when(step == n_steps - 1)
    def _():
        drain(slot)
        if n_steps > 1:
            drain(1 - slot)


def _dispatch_shared(dest_tiles, x, g, scale, shift, gate, sg, su, sd, x_disp):
    B, S, D = x.shape
    tm = COMB_TM
    n = S // tm
    row = lambda b, i: (b, i, 0)
    per_b = lambda b, i: (b, 0, 0)
    const2 = lambda b, i: (0, 0)
    return pl.pallas_call(
        functools.partial(_dispatch_kernel, n_steps=B * n),
        grid=(B, n),
        in_specs=[
            pl.BlockSpec((1, 1, TOP_K * tm), lambda b, i: (b * n + i, 0, 0), memory_space=pltpu.SMEM),
            pl.BlockSpec((1, tm, D), row),
            pl.BlockSpec((1, D), const2),
            pl.BlockSpec((1, 1, D), per_b),
            pl.BlockSpec((1, 1, D), per_b),
            pl.BlockSpec((1, 1, D), per_b),
            pl.BlockSpec((D, D_FF), const2),
            pl.BlockSpec((D, D_FF), const2),
            pl.BlockSpec((D_FF, D), const2),
            pl.BlockSpec(memory_space=pl.ANY),
        ],
        out_specs=[pl.BlockSpec((1, tm, D), row), pl.BlockSpec(memory_space=pl.ANY)],
        out_shape=[jax.ShapeDtypeStruct((B, S, D), F32), jax.ShapeDtypeStruct(x_disp.shape, F32)],
        input_output_aliases={9: 1},
        scratch_shapes=[pltpu.VMEM((2, tm, D), F32), pltpu.SemaphoreType.DMA((2,))],
        compiler_params=_cparams(("arbitrary", "arbitrary")),
        name="moe_dispatch",
    )(dest_tiles, x, g, scale, shift, gate, sg, su, sd, x_disp)


def _expert_kernel(be_ref, nu_ref, x_ref, wg_ref, wu_ref, wd_ref, o_ref):
    del be_ref

    @pl.when(pl.program_id(0) < nu_ref[0])
    def _():
        xb = x_ref[...].astype(BF16)
        act = _silu(_dot(xb, wg_ref[0])) * _dot(xb, wu_ref[0])
        o_ref[...] = _dot(act.astype(BF16), wd_ref[0])

    @pl.when(pl.program_id(0) >= nu_ref[0])
    def _():
        o_ref[...] = jnp.zeros_like(o_ref)


def _experts(block_expert, n_used, x_disp, wg, wu, wd):
    P, D = x_disp.shape
    bm = MOE_BM
    nb = P // bm
    blk = lambda i, be, nu: (jnp.minimum(i, jnp.maximum(nu[0] - 1, 0)), 0)
    wsel = lambda i, be, nu: (be[jnp.minimum(i, jnp.maximum(nu[0] - 1, 0))], 0, 0)
    grid_spec = pltpu.PrefetchScalarGridSpec(
        num_scalar_prefetch=2,
        grid=(nb,),
        in_specs=[
            pl.BlockSpec((bm, D), blk),
            pl.BlockSpec((1, D, D_FF), wsel),
            pl.BlockSpec((1, D, D_FF), wsel),
            pl.BlockSpec((1, D_FF, D), wsel),
        ],
        out_specs=pl.BlockSpec((bm, D), lambda i, be, nu: (i, 0)),
    )
    return pl.pallas_call(
        _expert_kernel,
        grid_spec=grid_spec,
        out_shape=jax.ShapeDtypeStruct((P, D), F32),
        compiler_params=_cparams(("arbitrary",)),
        name="experts",
    )(block_expert, n_used, x_disp, wg, wu, wd)


def _combine_kernel(pos_cur_ref, pos_next_ref, y_hbm, w_ref, xb_ref, gate_ref, o_ref, ybuf, sem):
    i = pl.program_id(0)
    n = pl.num_programs(0)
    slot = i % 2
    rows = ybuf.shape[1]
    tm = rows // TOP_K

    def issue(pos_ref, dst_slot):
        for r in range(rows):
            _row_copy(y_hbm, pos_ref[0, 0, r], ybuf.at[dst_slot], r, sem.at[dst_slot]).start()

    @pl.when(i == 0)
    def _():
        issue(pos_cur_ref, 0)

    @pl.when(i + 1 < n)
    def _():
        issue(pos_next_ref, 1 - slot)

    for _ in range(rows):
        _row_copy(y_hbm, 0, ybuf.at[slot], 0, sem.at[slot]).wait()
    w = w_ref[...]
    y = ybuf[slot, 0:tm, :] * w[:, 0:1] + ybuf[slot, tm:rows, :] * w[:, 1:2]
    o_ref[...] = xb_ref[...] + gate_ref[0] * y


def _combine(pos_tiles, y_disp, w_cols, xb, gate, tiles_per_seq):
    T, D = xb.shape
    tm = COMB_TM
    n = T // tm
    smem_blk = lambda f: pl.BlockSpec((1, 1, TOP_K * tm), f, memory_space=pltpu.SMEM)
    return pl.pallas_call(
        _combine_kernel,
        grid=(n,),
        in_specs=[
            smem_blk(lambda i: (i, 0, 0)),
            smem_blk(lambda i: (jnp.minimum(i + 1, n - 1), 0, 0)),
            pl.BlockSpec(memory_space=pl.ANY),
            pl.BlockSpec((tm, TOP_K), lambda i: (i, 0)),
            pl.BlockSpec((tm, D), lambda i: (i, 0)),
            pl.BlockSpec((1, 1, D), lambda i: (i // tiles_per_seq, 0, 0)),
        ],
        out_specs=pl.BlockSpec((tm, D), lambda i: (i, 0)),
        out_shape=jax.ShapeDtypeStruct((T, D), F32),
        scratch_shapes=[pltpu.VMEM((2, TOP_K * tm, D), F32), pltpu.SemaphoreType.DMA((2,))],
        compiler_params=_cparams(("arbitrary",)),
        name="combine",
    )(pos_tiles, pos_tiles, y_disp, w_cols, xb, gate)


def _axial_angles(S, hd):
    rows = S // GRID_W
    row = jnp.repeat(jnp.arange(rows), GRID_W).astype(F32)
    col = jnp.tile(jnp.arange(GRID_W), rows).astype(F32)
    n = hd // 4
    inv = ROPE_THETA ** (-jnp.arange(n, dtype=F32) / n)
    ang = jnp.concatenate([row[:, None] * inv, col[:, None] * inv], axis=-1)
    return jnp.cos(ang), jnp.sin(ang)


def _block_diag_gates(gate_w):
    per = LANES // RG_BW
    n_lg = D_RNN // LANES
    g = gate_w.reshape(2, 2, n_lg, per, RG_BW, RG_BW)
    eye = jnp.eye(per, dtype=gate_w.dtype)
    bd = jnp.einsum('dglpij,pq->dglpiqj', g, eye)
    return bd.reshape(2, 2, n_lg, LANES, LANES).astype(BF16)


def kernel(x_prompt, x_sample, c_prompt, c_sample, norm_mix, norm_ffn, w_ada, b_ada, w_in_ab, conv_w,
           conv_b, rg_gate_w, rg_gate_b, rg_lambda, q_norm, k_norm, w_out_ab, w_in_ret, w_out_ret,
           router_w, router_bias, exp_w_gate, exp_w_up, exp_w_down, sh_w_gate, sh_w_up, sh_w_down):
    assert x_prompt.shape[1:] == x_sample.shape[1:]
    Bp_, S, D = x_prompt.shape
    B = Bp_ + x_sample.shape[0]
    T = B * S
    x = jnp.concatenate([x_prompt, x_sample], axis=0)
    c = jnp.concatenate([c_prompt, c_sample], axis=0)
    c_rows = -(-B // 16) * 16
    c_pad = jnp.zeros((c_rows, D), F32).at[:B].set(c)
    mod = _ada_mod(c_pad, w_ada, b_ada)[:, :B].reshape(DEPTH, B, 6, 1, D)

    cos_a, sin_a = _axial_angles(S, HEAD_DIM)
    cos_att = jnp.tile(jnp.concatenate([cos_a, cos_a], axis=-1), (1, ATT_HEADS))
    sin_att = jnp.tile(jnp.concatenate([-sin_a, sin_a], axis=-1), (1, ATT_HEADS))
    cos_r, sin_r = _axial_angles(S, RET_QK)
    seg = jnp.arange(D_ATT) // HEAD_DIM
    ones_bd = (seg[:, None] == seg[None, :]).astype(BF16)
    tabs_f = _shape_tables(*_retention_tables(0.0, False), flip=False)
    tabs_b = _shape_tables(*_retention_tables(0.5, True), flip=True)
    rwt = router_w.T.astype(BF16)
    rb = router_bias.reshape(N_EXPERTS, 1).astype(F32)

    for l in range(DEPTH):
        shift1, scale1, gate1, shift2, scale2, gate2 = (mod[l, :, j] for j in range(6))
        g_mix = norm_mix[l].reshape(1, D)
        if l % 2 == 0:
            e = l // 2
            rg, q, k, v = _inproj_even(
                x, g_mix, scale1, shift1, w_in_ab[e].astype(BF16), ones_bd,
                jnp.tile(q_norm[e], ATT_HEADS).reshape(1, D_ATT),
                jnp.tile(k_norm[e], KV_HEADS).reshape(1, KV_HEADS * HEAD_DIM), cos_att, sin_att)
            y_rg = _rglru(rg, conv_w[e], conv_b[e].reshape(1, D_RNN), _block_diag_gates(rg_gate_w[e]),
                          rg_gate_b[e].reshape(2, 2, 1, D_RNN), rg_lambda[e].reshape(2, 1, D_RNN))
            y_att = _attention(q, k, v)
            ys, w_out = [y_rg, y_att], w_out_ab[e]
        else:
            o = l // 2
            q, k, v, gate = _inproj_ret(x, g_mix, scale1, shift1, w_in_ret[o].astype(BF16), cos_r, sin_r)
            kt = k.reshape(B, S, RET_HEADS, RET_QK).transpose(0, 2, 3, 1)
            y_f = _retention_pass(q, kt, v, tabs_f, reverse=False)
            y_n = _retention_pass(q, kt, v, tabs_b, reverse=True, yf=y_f, gate=gate)
            ys, w_out = [y_n], w_out_ret[o]

        g_ffn = norm_ffn[l].reshape(1, D)
        x, idx, wts, rank, counts = _outproj_route(
            ys, w_out.astype(BF16), x, gate1, g_ffn, scale2, shift2, rwt, rb)
        block_expert, n_used, tail_block, dest_tiles = _dispatch_plan(idx, rank, counts)
        x_disp = _zero_tail_blocks(tail_block, T * TOP_K + N_EXPERTS * MOE_BM, D)
        xb, x_disp = _dispatch_shared(
            dest_tiles, x, g_ffn, scale2, shift2, gate2,
            sh_w_gate[l].astype(BF16), sh_w_up[l].astype(BF16), sh_w_down[l].astype(BF16), x_disp)
        y_disp = _experts(block_expert, n_used, x_disp,
                          exp_w_gate[l].astype(BF16), exp_w_up[l].astype(BF16), exp_w_down[l].astype(BF16))
        x = _combine(dest_tiles, y_disp, wts.T, xb.reshape(T, D), gate2, S // COMB_TM).reshape(B, S, D)

    return x[:Bp_], x[Bp_:]
```
